```python
import jax, jax.numpy as jnp
from jax import lax
import numpy as np

D_MODEL = 4096
BATCH = 4
SEQ = 2048
DEPTH = 2
DEC_BATCH = 32
DEC_SEQ = 4
PAST_LEN = 16384
PAGE_SIZE = 128

N_A = DEPTH // 2
N_B = DEPTH - N_A
HD_A = 128
A_HG = 16
A_CONFIGS = ((128, 1), (512, 4), (2048, 16))
A_GROUPS = len(A_CONFIGS)
HD_B = 64
H_B = D_MODEL // HD_B
KV_B = 8
WINDOW_B = 128
D_FF = 11008
CONV_W = 3
ROPE_THETA = 500000.0
ROT_FRAC = 4
NORM_EPS = 1e-5
BLOCK = 128
NEG_INF = -1e30

kernel_name = "yoco_dilated_swa_sink_convffn_step"


def rms_norm(x, g):
    xf = x.astype(jnp.float32)
    y = xf * lax.rsqrt(jnp.mean(xf * xf, axis=-1, keepdims=True) + NORM_EPS)
    return (y * g.astype(jnp.float32)).astype(x.dtype)


def partial_rope(x, pos):
    hd = x.shape[-1]
    rot = hd // ROT_FRAC
    half = rot // 2
    inv = ROPE_THETA ** (-jnp.arange(half, dtype=jnp.float32) * 2.0 / rot)
    ang = pos.astype(jnp.float32)[:, None] * inv[None, :]
    shape = (ang.shape[0],) + (1,) * (x.ndim - 3) + (half,)
    cos = jnp.cos(ang).reshape(shape).astype(x.dtype)
    sin = jnp.sin(ang).reshape(shape).astype(x.dtype)
    x1, x2, rest = x[..., :half], x[..., half:rot], x[..., rot:]
    return jnp.concatenate([x1 * cos - x2 * sin, x2 * cos + x1 * sin, rest], axis=-1)


def lse_with_sink(s, sink):
    m = jnp.maximum(jnp.max(s, axis=-1), sink)
    return m + jnp.log(jnp.sum(jnp.exp(s - m[..., None]), axis=-1) + jnp.exp(sink - m))


def banded_attention(q, k, v, span, sinks):
    n, l, hq, hd = q.shape
    hkv = k.shape[2]
    g = hq // hkv
    nb = -(-l // BLOCK)
    pad = nb * BLOCK - l
    padw = ((0, 0), (0, pad), (0, 0), (0, 0))
    q, k, v = jnp.pad(q, padw), jnp.pad(k, padw), jnp.pad(v, padw)
    qb = q.reshape(n, nb, BLOCK, hkv, g, hd)
    kb = k.reshape(n, nb, BLOCK, hkv, hd)
    vb = v.reshape(n, nb, BLOCK, hkv, hd)
    zk = jnp.zeros_like(kb[:, :1])
    kk = jnp.concatenate([jnp.concatenate([zk, kb[:, :-1]], axis=1), kb], axis=2)
    vv = jnp.concatenate([jnp.concatenate([zk, vb[:, :-1]], axis=1), vb], axis=2)
    s = jnp.einsum('nbqhgd,nbkhd->nbhgqk', qb, kk).astype(jnp.float32) * (hd ** -0.5)
    qi = jnp.arange(BLOCK)[:, None]
    kj = jnp.arange(2 * BLOCK)[None, :]
    dist = qi + BLOCK - kj
    blk = jnp.arange(nb)[:, None, None]
    valid = (dist >= 0) & (dist <= span) & ((blk > 0) | (kj >= BLOCK))
    s = jnp.where(valid[None, :, None, None], s, NEG_INF)
    if sinks is None:
        lse = jax.nn.logsumexp(s, axis=-1)
    else:
        lse = lse_with_sink(s, sinks.astype(jnp.float32).reshape(hkv, g, 1))
    p = jnp.exp(s - lse[..., None]).astype(v.dtype)
    o = jnp.einsum('nbhgqk,nbkhd->nbqhgd', p, vv).reshape(n, nb * BLOCK, hq, hd)[:, :l]
    lse = lse.transpose(0, 1, 4, 2, 3).reshape(n, nb * BLOCK, hq)[:, :l]
    return o, lse


def dilated_prompt(q, k, v, window, dil):
    b, s, h, hd = q.shape
    m = s // dil

    def to_res(t):
        return t.reshape(b, m, dil, h, hd).transpose(0, 2, 1, 3, 4).reshape(b * dil, m, h, hd)

    o, lse = banded_attention(to_res(q), to_res(k), to_res(v), window // dil, None)
    o = o.reshape(b, dil, m, h, hd).transpose(0, 2, 1, 3, 4).reshape(b, s, h, hd)
    lse = lse.reshape(b, dil, m, h).transpose(0, 2, 1, 3).reshape(b, s, h)
    return o, lse


def dilated_sample(q, kc, vc, window, dil):
    t, hd = q.shape[1], q.shape[-1]
    lb = kc.shape[1] - t
    dists = jnp.arange(window // dil + 1) * dil
    idx = lb + jnp.arange(t)[:, None] - dists[None, :]
    valid = idx >= 0
    idx = jnp.maximum(idx, 0)
    kg = kc[:, idx]
    vg = vc[:, idx]
    s = jnp.einsum('bthd,btkhd->bthk', q, kg).astype(jnp.float32) * (hd ** -0.5)
    s = jnp.where(valid[None, :, None, :], s, NEG_INF)
    lse = jax.nn.logsumexp(s, axis=-1)
    p = jnp.exp(s - lse[..., None]).astype(vc.dtype)
    return jnp.einsum('bthk,btkhd->bthd', p, vg), lse


def window_sample(q, kc, vc, span, sinks):
    db, t, hq, hd = q.shape
    hkv = kc.shape[2]
    g = hq // hkv
    lb = kc.shape[1] - t
    qg = q.reshape(db, t, hkv, g, hd)
    s = jnp.einsum('bthgd,bkhd->bhgtk', qg, kc).astype(jnp.float32) * (hd ** -0.5)
    dist = lb + jnp.arange(t)[:, None] - jnp.arange(lb + t)[None, :]
    valid = (dist >= 0) & (dist <= span)
    s = jnp.where(valid, s, NEG_INF)
    lse = lse_with_sink(s, sinks.astype(jnp.float32).reshape(hkv, g, 1))
    p = jnp.exp(s - lse[..., None]).astype(vc.dtype)
    return jnp.einsum('bhgtk,bkhd->bthgd', p, vc).reshape(db, t, hq, hd)


def merge_dilations(outs, lses):
    lse = jnp.stack(lses, axis=0)
    w = jnp.exp(lse - jax.nn.logsumexp(lse, axis=0, keepdims=True))
    o = jnp.stack(outs, axis=0)
    return jnp.sum(w[..., None].astype(o.dtype) * o, axis=0)


def project_qkv_a(x, pos, w_qkv):
    b, s, _ = x.shape
    qkv = (x @ w_qkv).reshape(b, s, 3, A_GROUPS, A_HG, HD_A)
    q = partial_rope(qkv[:, :, 0], pos)
    k = partial_rope(qkv[:, :, 1], pos)
    return q, k, qkv[:, :, 2]


def mixer_a_prompt(x, pos, w_qkv, w_o):
    b, s, _ = x.shape
    q, k, v = project_qkv_a(x, pos, w_qkv)
    outs, lses, new = [], [], []
    for gi, (win, dil) in enumerate(A_CONFIGS):
        o, l = dilated_prompt(q[:, :, gi], k[:, :, gi], v[:, :, gi], win, dil)
        outs.append(o)
        lses.append(l)
        new.append(jnp.stack([k[:, :, gi], v[:, :, gi]], axis=2)[:, -min(win, s):])
    merged = merge_dilations(outs, lses).reshape(b, s, A_HG * HD_A)
    return merged @ w_o, new


def mixer_a_sample(x, pos, caches, w_qkv, w_o):
    b, t, _ = x.shape
    q, k, v = project_qkv_a(x, pos, w_qkv)
    outs, lses, new = [], [], []
    for gi, (win, dil) in enumerate(A_CONFIGS):
        kvc = jnp.concatenate([caches[gi].astype(x.dtype), jnp.stack([k[:, :, gi], v[:, :, gi]], axis=2)], axis=1)
        o, l = dilated_sample(q[:, :, gi], kvc[:, :, 0], kvc[:, :, 1], win, dil)
        outs.append(o)
        lses.append(l)
        new.append(kvc[:, t:])
    merged = merge_dilations(outs, lses).reshape(b, t, A_HG * HD_A)
    return merged @ w_o, new


def shared_kv(x, pos, g_kv, w_kv):
    b, s, _ = x.shape
    kv = (rms_norm(x, g_kv) @ w_kv).reshape(b, s, 2, KV_B, HD_B)
    return jnp.stack([partial_rope(kv[:, :, 0], pos), kv[:, :, 1]], axis=2)


def query_b(x, pos, w_q):
    b, s, _ = x.shape
    return partial_rope((x @ w_q).reshape(b, s, H_B, HD_B), pos)


def conv_ffn(x, prev, w_gate, w_up, conv_w, conv_b, w_down):
    s = x.shape[1]
    g = x @ w_gate
    u = x @ w_up
    gc = jnp.concatenate([prev.astype(g.dtype), g], axis=1)
    c = conv_b
    for j in range(CONV_W):
        c = c + conv_w[j] * gc[:, j:j + s]
    return (jax.nn.silu(c) * u) @ w_down, gc[:, -(CONV_W - 1):]


def setup_inputs(seed: int = 0) -> dict:
    key = jax.random.key(seed)
    ks = jax.random.split(key, 32)

    def nrm(k, shape, scale):
        return jax.random.normal(k, shape, jnp.float32) * scale

    d_attn_a = A_HG * HD_A
    d_q_b = H_B * HD_B
    return {
        "x_prompt": nrm(ks[0], (BATCH, SEQ, D_MODEL), 1.0),
        "x_sample": nrm(ks[1], (DEC_BATCH, DEC_SEQ, D_MODEL), 1.0),
        "cache_a1": nrm(ks[2], (N_A, DEC_BATCH, min(A_CONFIGS[0][0], PAST_LEN), 2, A_HG, HD_A), 1.0),
        "cache_a2": nrm(ks[3], (N_A, DEC_BATCH, min(A_CONFIGS[1][0], PAST_LEN), 2, A_HG, HD_A), 1.0),
        "cache_a3": nrm(ks[4], (N_A, DEC_BATCH, min(A_CONFIGS[2][0], PAST_LEN), 2, A_HG, HD_A), 1.0),
        "cache_b": nrm(ks[5], (DEC_BATCH, min(WINDOW_B, PAST_LEN), 2, KV_B, HD_B), 1.0),
        "state_conv": nrm(ks[6], (DEPTH, DEC_BATCH, CONV_W - 1, D_FF), 1.0),
        "norm_attn": 1.0 + nrm(ks[7], (DEPTH, D_MODEL), 0.05),
        "norm_ffn": 1.0 + nrm(ks[8], (DEPTH, D_MODEL), 0.05),
        "w_qkv_a": nrm(ks[9], (N_A, D_MODEL, 3 * A_GROUPS * A_HG * HD_A), D_MODEL ** -0.5),
        "w_o_a": nrm(ks[10], (N_A, d_attn_a, D_MODEL), d_attn_a ** -0.5),
        "norm_kv": 1.0 + nrm(ks[11], (D_MODEL,), 0.05),
        "w_kv_b": nrm(ks[12], (D_MODEL, 2 * KV_B * HD_B), D_MODEL ** -0.5),
        "w_q_b": nrm(ks[13], (N_B, D_MODEL, d_q_b), D_MODEL ** -0.5),
        "sinks_b": nrm(ks[14], (N_B, H_B), 0.5),
        "w_o_b": nrm(ks[15], (N_B, d_q_b, D_MODEL), d_q_b ** -0.5),
        "w_gate": nrm(ks[16], (DEPTH, D_MODEL, D_FF), D_MODEL ** -0.5),
        "w_up": nrm(ks[17], (DEPTH, D_MODEL, D_FF), D_MODEL ** -0.5),
        "conv_w": nrm(ks[18], (DEPTH, CONV_W, D_FF), CONV_W ** -0.5),
        "conv_b": nrm(ks[19], (DEPTH, D_FF), 0.01),
        "w_down": nrm(ks[20], (DEPTH, D_FF, D_MODEL), D_FF ** -0.5),
        "norm_final": 1.0 + nrm(ks[21], (D_MODEL,), 0.05),
    }


def reference(x_prompt, x_sample, cache_a1, cache_a2, cache_a3, cache_b, state_conv,
              norm_attn, norm_ffn, w_qkv_a, w_o_a, norm_kv, w_kv_b, w_q_b, sinks_b, w_o_b,
              w_gate, w_up, conv_w, conv_b, w_down, norm_final):
    pos_p = jnp.arange(x_prompt.shape[1], dtype=jnp.int32)
    pos_s = PAST_LEN + jnp.arange(x_sample.shape[1], dtype=jnp.int32)
    hp, hs = x_prompt, x_sample
    a_caches = (cache_a1, cache_a2, cache_a3)
    new_a_p = [[] for _ in range(A_GROUPS)]
    new_a_s = [[] for _ in range(A_GROUPS)]
    conv_p, conv_s = [], []
    kv_p = kvc_s = None
    for layer in range(DEPTH):
        xp = rms_norm(hp, norm_attn[layer])
        xs = rms_norm(hs, norm_attn[layer])
        if layer < N_A:
            ap, newp = mixer_a_prompt(xp, pos_p, w_qkv_a[layer], w_o_a[layer])
            as_, news = mixer_a_sample(xs, pos_s, [c[layer] for c in a_caches], w_qkv_a[layer], w_o_a[layer])
            for gi in range(A_GROUPS):
                new_a_p[gi].append(newp[gi])
                new_a_s[gi].append(news[gi])
        else:
            if layer == N_A:
                kv_p = shared_kv(hp, pos_p, norm_kv, w_kv_b)
                kvc_s = jnp.concatenate([cache_b.astype(hs.dtype), shared_kv(hs, pos_s, norm_kv, w_kv_b)], axis=1)
            lb = layer - N_A
            qp = query_b(xp, pos_p, w_q_b[lb])
            qs = query_b(xs, pos_s, w_q_b[lb])
            op, _ = banded_attention(qp, kv_p[:, :, 0], kv_p[:, :, 1], WINDOW_B, sinks_b[lb])
            os_ = window_sample(qs, kvc_s[:, :, 0], kvc_s[:, :, 1], WINDOW_B, sinks_b[lb])
            ap = op.reshape(hp.shape[0], hp.shape[1], H_B * HD_B) @ w_o_b[lb]
            as_ = os_.reshape(hs.shape[0], hs.shape[1], H_B * HD_B) @ w_o_b[lb]
        hp = hp + ap
        hs = hs + as_
        zeros_p = jnp.zeros((hp.shape[0], CONV_W - 1, D_FF), hp.dtype)
        fp, cp = conv_ffn(rms_norm(hp, norm_ffn[layer]), zeros_p, w_gate[layer], w_up[layer], conv_w[layer], conv_b[layer], w_down[layer])
        fs, cs = conv_ffn(rms_norm(hs, norm_ffn[layer]), state_conv[layer], w_gate[layer], w_up[layer], conv_w[layer], conv_b[layer], w_down[layer])
        conv_p.append(cp)
        conv_s.append(cs)
        hp = hp + fp
        hs = hs + fs
    y_prompt = rms_norm(hp, norm_final)
    y_sample = rms_norm(hs, norm_final)
    new_a1_p = jnp.stack(new_a_p[0], axis=0)
    new_a1_s = jnp.stack(new_a_s[0], axis=0)
    new_a2_p = jnp.stack(new_a_p[1], axis=0)
    new_a2_s = jnp.stack(new_a_s[1], axis=0)
    new_a3_p = jnp.stack(new_a_p[2], axis=0)
    new_a3_s = jnp.stack(new_a_s[2], axis=0)
    new_b_p = kv_p[:, -min(WINDOW_B, kv_p.shape[1]):]
    new_b_s = kvc_s[:, x_sample.shape[1]:]
    new_conv_p = jnp.stack(conv_p, axis=0)
    new_conv_s = jnp.stack(conv_s, axis=0)
    return (y_prompt, y_sample, new_a1_p, new_a1_s, new_a2_p, new_a2_s, new_a3_p, new_a3_s, new_b_p, new_b_s, new_conv_p, new_conv_s)
```

```python
import functools

import jax
import jax.numpy as jnp
from jax import lax
from jax.experimental import pallas as pl
from jax.experimental.pallas import tpu as pltpu

F32 = jnp.float32
BF16 = jnp.bfloat16

HD_A = 128
A_HG = 16
A_CONFIGS = ((128, 1), (512, 4), (2048, 16))
A_GROUPS = len(A_CONFIGS)
HD_B = 64
KV_B = 8
WINDOW_B = 128
PAST_LEN = 16384
CONV_W = 3
ROPE_THETA = 500000.0
ROT_FRAC = 4
NORM_EPS = 1e-5
BLOCK = 128
NEG_INF = -1e30
LANES = 128
FF_ALIGN = 1024
VMEM_LIMIT = 56 * 1024 * 1024


def _params(*sem):
    return pltpu.CompilerParams(dimension_semantics=sem, vmem_limit_bytes=VMEM_LIMIT)


def _rmsnorm_kernel(h_ref, g_ref, *o_refs):
    x = h_ref[...]
    y = x * lax.rsqrt(jnp.mean(x * x, axis=-1, keepdims=True) + NORM_EPS)
    for n, o_ref in enumerate(o_refs):
        o_ref[...] = (y * g_ref[n:n + 1, :]).astype(o_ref.dtype)


def rmsnorm(h, gains, out_dtype, tm):
    m, d = h.shape
    n = gains.shape[0]
    return pl.pallas_call(
        _rmsnorm_kernel,
        out_shape=[jax.ShapeDtypeStruct((m, d), out_dtype)] * n,
        grid=(m // tm,),
        in_specs=[pl.BlockSpec((tm, d), lambda i: (i, 0)),
                  pl.BlockSpec((n, d), lambda i: (0, 0))],
        out_specs=[pl.BlockSpec((tm, d), lambda i: (i, 0))] * n,
        compiler_params=_params("parallel"),
        name="rmsnorm",
    )(h, gains)


def rope_tables(pos, hd):
    rot = hd // ROT_FRAC
    half = rot // 2
    inv = ROPE_THETA ** (-jnp.arange(half, dtype=F32) * 2.0 / rot)
    ang = pos.astype(F32)[:, None] * inv[None, :]
    cos, sin = jnp.cos(ang), jnp.sin(ang)
    s = pos.shape[0]
    ones = jnp.ones((s, hd - rot), F32)
    zrest = jnp.zeros((s, hd - rot), F32)
    zhalf = jnp.zeros((s, half), F32)
    reps = LANES // hd
    cos_t = jnp.tile(jnp.concatenate([cos, cos, ones], axis=1), (1, reps))
    sa_t = jnp.tile(jnp.concatenate([-sin, zhalf, zrest], axis=1), (1, reps))
    sb_t = jnp.tile(jnp.concatenate([zhalf, sin, zrest], axis=1), (1, reps))
    return cos_t, sa_t, sb_t


def _proj_kernel(x_ref, w_ref, cos_ref, sa_ref, sb_ref, *o_refs, nj, n_rope, f32_from, shift):
    j = pl.program_id(1)
    acc = jnp.dot(x_ref[...], w_ref[...], preferred_element_type=F32)
    obf_ref = o_refs[0]
    of32_ref = o_refs[1] if len(o_refs) > 1 else None
    tn = acc.shape[1]

    def write(roped, keep_f32):
        for c in range(tn // LANES):
            sl = slice(c * LANES, (c + 1) * LANES)
            xh = acc[:, sl]
            if roped:
                xh = (xh * cos_ref[...] + pltpu.roll(xh, LANES - shift, 1) * sa_ref[...]
                      + pltpu.roll(xh, shift, 1) * sb_ref[...])
            obf_ref[:, sl] = xh.astype(obf_ref.dtype)
            if keep_f32:
                of32_ref[:, sl] = xh

    for roped in (True, False):
        for keep_f32 in (True, False):
            if keep_f32 and of32_ref is None:
                continue
            lo = 0 if roped else n_rope
            hi = n_rope if roped else nj
            if keep_f32:
                lo = max(lo, f32_from)
            elif of32_ref is not None:
                hi = min(hi, f32_from)
            if hi <= lo:
                continue
            pl.when((j >= lo) & (j < hi))(functools.partial(write, roped, keep_f32))


def proj(xn, w, tables, *, col0, ncols, tm, tn, pos_tiles, n_rope, f32_from, shift):
    m, k = xn.shape
    nj = ncols // tn
    j0 = col0 // tn
    keep = f32_from is not None
    out_shape = [jax.ShapeDtypeStruct((m, ncols), BF16)]
    out_specs = [pl.BlockSpec((tm, tn), lambda i, j: (i, j))]
    if keep:
        out_shape.append(jax.ShapeDtypeStruct((m, ncols - f32_from * tn), F32))
        out_specs.append(pl.BlockSpec((tm, tn), lambda i, j: (i, jnp.maximum(j - f32_from, 0))))
    tab_spec = pl.BlockSpec((tm, LANES), lambda i, j: (i % pos_tiles, 0))
    outs = pl.pallas_call(
        functools.partial(_proj_kernel, nj=nj, n_rope=n_rope, f32_from=f32_from if keep else nj, shift=shift),
        out_shape=out_shape,
        grid=(m // tm, nj),
        in_specs=[pl.BlockSpec((tm, k), lambda i, j: (i, 0)),
                  pl.BlockSpec((k, tn), lambda i, j: (0, j0 + j)),
                  tab_spec, tab_spec, tab_spec],
        out_specs=out_specs,
        compiler_params=_params("arbitrary", "arbitrary"),
        name="proj",
    )(xn, w, *tables)
    return outs if keep else outs[0]


def _mm_res_kernel(x_ref, w_ref, res_ref, o_ref, *scratch, nk):
    part = jnp.dot(x_ref[...], w_ref[...], preferred_element_type=F32)
    if nk == 1:
        o_ref[...] = res_ref[...] + part
        return
    acc_ref, = scratch
    k = pl.program_id(2)

    @pl.when(k == 0)
    def _():
        acc_ref[...] = part

    @pl.when((k > 0) & (k < nk - 1))
    def _():
        acc_ref[...] += part

    @pl.when(k == nk - 1)
    def _():
        o_ref[...] = res_ref[...] + (acc_ref[...] + part)


def mm_res(x, w, res, *, tm, tn, tk):
    m, kdim = x.shape
    n = w.shape[1]
    nk = kdim // tk
    return pl.pallas_call(
        functools.partial(_mm_res_kernel, nk=nk),
        out_shape=jax.ShapeDtypeStruct((m, n), F32),
        grid=(m // tm, n // tn, nk),
        in_specs=[pl.BlockSpec((tm, tk), lambda i, j, k: (i, k)),
                  pl.BlockSpec((tk, tn), lambda i, j, k: (k, j)),
                  pl.BlockSpec((tm, tn), lambda i, j, k: (i, j))],
        out_specs=pl.BlockSpec((tm, tn), lambda i, j, k: (i, j)),
        scratch_shapes=[pltpu.VMEM((tm, tn), F32)] if nk > 1 else [],
        compiler_params=_params("parallel", "parallel", "arbitrary"),
        name="mm_res",
    )(x, w, res)


def _silu(c):
    return c * jax.nn.sigmoid(c)


def _ffn_up_prompt_kernel(x_ref, wg_ref, wu_ref, cw_ref, cb_ref, act_ref, tail_ref, carry_ref, *, tiles_per_seq):
    i = pl.program_id(0)
    j = pl.program_id(1)
    x = x_ref[...]
    g = jnp.dot(x, wg_ref[...], preferred_element_type=F32)
    u = jnp.dot(x, wu_ref[...], preferred_element_type=F32)
    tm = g.shape[0]

    @pl.when((i % tiles_per_seq) == 0)
    def _():
        carry_ref[j] = jnp.zeros(carry_ref.shape[1:], F32)

    prev = carry_ref[j]
    last = g[tm - 8:, :]
    carry_ref[j] = last
    tail_ref[...] = last
    row = lax.broadcasted_iota(jnp.int32, (tm, 1), 0)
    g1 = jnp.where(row == 0, prev[7:8, :], pltpu.roll(g, 1, 0))
    g2 = jnp.where(row == 0, prev[6:7, :], jnp.where(row == 1, prev[7:8, :], pltpu.roll(g, 2, 0)))
    c = cb_ref[...] + cw_ref[0:1, :] * g2 + cw_ref[1:2, :] * g1 + cw_ref[2:3, :] * g
    act_ref[...] = (_silu(c) * u).astype(act_ref.dtype)


def ffn_up_prompt(xn, wg, wu, cw, cb, *, tm, tf, tiles_per_seq):
    m, d = xn.shape
    f = wg.shape[1]
    nj = f // tf
    return pl.pallas_call(
        functools.partial(_ffn_up_prompt_kernel, tiles_per_seq=tiles_per_seq),
        out_shape=[jax.ShapeDtypeStruct((m, f), BF16),
                   jax.ShapeDtypeStruct((m // tm, 8, f), F32)],
        grid=(m // tm, nj),
        in_specs=[pl.BlockSpec((tm, d), lambda i, j: (i, 0)),
                  pl.BlockSpec((d, tf), lambda i, j: (0, j)),
                  pl.BlockSpec((d, tf), lambda i, j: (0, j)),
                  pl.BlockSpec((CONV_W, tf), lambda i, j: (0, j)),
                  pl.BlockSpec((1, tf), lambda i, j: (0, j))],
        out_specs=[pl.BlockSpec((tm, tf), lambda i, j: (i, j)),
                   pl.BlockSpec((None, 8, tf), lambda i, j: (i, 0, j))],
        scratch_shapes=[pltpu.VMEM((nj, 8, tf), F32)],
        compiler_params=_params("arbitrary", "arbitrary"),
        name="ffn_up_prompt",
    )(xn, wg, wu, cw, cb)


def _ffn_up_sample_kernel(x_ref, wg_ref, wu_ref, cw_ref, cb_ref, prev_ref, act_ref, tail_ref, *, nb):
    x = x_ref[...]
    g = jnp.dot(x, wg_ref[...], preferred_element_type=F32)
    u = jnp.dot(x, wu_ref[...], preferred_element_type=F32)
    m = g.shape[0]
    gext = jnp.concatenate([prev_ref[...], g], axis=0)
    c = (cb_ref[...] + cw_ref[0:1, :] * gext[0:m] + cw_ref[1:2, :] * gext[nb:nb + m]
         + cw_ref[2:3, :] * gext[2 * nb:2 * nb + m])
    act_ref[...] = (_silu(c) * u).astype(act_ref.dtype)
    tail_ref[...] = gext[m:, :]


def ffn_up_sample(xn, wg, wu, cw, cb, prev, *, tf, nb):
    m, d = xn.shape
    f = wg.shape[1]
    return pl.pallas_call(
        functools.partial(_ffn_up_sample_kernel, nb=nb),
        out_shape=[jax.ShapeDtypeStruct((m, f), BF16),
                   jax.ShapeDtypeStruct((2 * nb, f), F32)],
        grid=(f // tf,),
        in_specs=[pl.BlockSpec((m, d), lambda j: (0, 0)),
                  pl.BlockSpec((d, tf), lambda j: (0, j)),
                  pl.BlockSpec((d, tf), lambda j: (0, j)),
                  pl.BlockSpec((CONV_W, tf), lambda j: (0, j)),
                  pl.BlockSpec((1, tf), lambda j: (0, j)),
                  pl.BlockSpec((2 * nb, tf), lambda j: (0, j))],
        out_specs=[pl.BlockSpec((m, tf), lambda j: (0, j)),
                   pl.BlockSpec((2 * nb, tf), lambda j: (0, j))],
        compiler_params=_params("parallel"),
        name="ffn_up_sample",
    )(xn, wg, wu, cw, cb, prev)


def _attn_a_prompt_kernel(*refs, use_prev, has_in, final, scale):
    it = iter(refs)
    q_ref = next(it)
    kc_ref = next(it)
    kp_ref = next(it) if use_prev else None
    vc_ref = next(it)
    vp_ref = next(it) if use_prev else None
    oin_ref = next(it) if has_in else None
    lin_ref = next(it) if has_in else None
    oout_ref = next(it)
    lout_ref = None if final else next(it)

    mb = pl.program_id(2)
    nkeys = 2 * BLOCK if use_prev else BLOCK
    qi = lax.broadcasted_iota(jnp.int32, (BLOCK, nkeys), 0)
    kj = lax.broadcasted_iota(jnp.int32, (BLOCK, nkeys), 1)
    if use_prev:
        dist = qi + BLOCK - kj
        valid = (dist >= 0) & (dist <= BLOCK) & ((mb > 0) | (kj >= BLOCK))
    else:
        valid = qi >= kj
    lane = lax.broadcasted_iota(jnp.int32, (BLOCK, LANES), 1)
    lse_tile = jnp.zeros((BLOCK, LANES), F32)
    for h in range(A_HG):
        sl = slice(h * HD_A, (h + 1) * HD_A)
        q = q_ref[:, sl]
        if use_prev:
            k = jnp.concatenate([kp_ref[:, sl], kc_ref[:, sl]], axis=0)
            v = jnp.concatenate([vp_ref[:, sl], vc_ref[:, sl]], axis=0)
        else:
            k = kc_ref[:, sl]
            v = vc_ref[:, sl]
        s = lax.dot_general(q, k, (((1,), (1,)), ((), ())), preferred_element_type=F32) * scale
        s = jnp.where(valid, s, NEG_INF)
        m = jnp.max(s, axis=1, keepdims=True)
        p = jnp.exp(s - m)
        l = jnp.sum(p, axis=1, keepdims=True)
        o = jnp.dot(p.astype(BF16), v, preferred_element_type=F32) / l
        lse = m + jnp.log(l)
        if has_in:
            lse_in = lin_ref[:, h:h + 1]
            top = jnp.maximum(lse_in, lse)
            lse_new = top + jnp.log(jnp.exp(lse_in - top) + jnp.exp(lse - top))
            o = jnp.exp(lse_in - lse_new) * oin_ref[:, sl] + jnp.exp(lse - lse_new) * o
            lse = lse_new
        oout_ref[:, sl] = o.astype(oout_ref.dtype)
        if not final:
            lse_tile = jnp.where(lane == h, lse, lse_tile)
    if not final:
        lout_ref[...] = lse_tile


def attn_a_prompt(qkv, *, nb, seq):
    m_rows = nb * seq
    gw = A_HG * HD_A
    nseg = 3 * A_GROUPS
    o_state = None
    lse_state = None
    for gi, (win, dil) in enumerate(A_CONFIGS):
        assert win // dil == BLOCK and seq % (dil * BLOCK) == 0
        mlen = seq // dil
        nblk = mlen // BLOCK
        use_prev = nblk > 1
        has_in = gi > 0
        final = gi == A_GROUPS - 1
        qkv_v = qkv.reshape(nb, mlen, dil * nseg * gw)

        def seg_spec(seg, prev):
            if prev:
                return pl.BlockSpec((None, BLOCK, gw),
                                    lambda b, r, mb, seg=seg: (b, jnp.maximum(mb - 1, 0), r * nseg + seg))
            return pl.BlockSpec((None, BLOCK, gw), lambda b, r, mb, seg=seg: (b, mb, r * nseg + seg))

        o_spec = pl.BlockSpec((None, BLOCK, gw), lambda b, r, mb: (b, mb, r))
        l_spec = pl.BlockSpec((None, BLOCK, LANES), lambda b, r, mb: (b, mb, r))
        args = [qkv_v, qkv_v]
        in_specs = [seg_spec(gi, False), seg_spec(A_GROUPS + gi, False)]
        if use_prev:
            args.append(qkv_v)
            in_specs.append(seg_spec(A_GROUPS + gi, True))
        args.append(qkv_v)
        in_specs.append(seg_spec(2 * A_GROUPS + gi, False))
        if use_prev:
            args.append(qkv_v)
            in_specs.append(seg_spec(2 * A_GROUPS + gi, True))
        if has_in:
            args += [o_state.reshape(nb, mlen, dil * gw), lse_state.reshape(nb, mlen, dil * LANES)]
            in_specs += [o_spec, l_spec]
        out_shape = [jax.ShapeDtypeStruct((nb, mlen, dil * gw), BF16 if final else F32)]
        out_specs = [o_spec]
        if not final:
            out_shape.append(jax.ShapeDtypeStruct((nb, mlen, dil * LANES), F32))
            out_specs.append(l_spec)
        outs = pl.pallas_call(
            functools.partial(_attn_a_prompt_kernel, use_prev=use_prev, has_in=has_in, final=final,
                              scale=HD_A ** -0.5),
            out_shape=out_shape,
            grid=(nb, dil, nblk),
            in_specs=in_specs,
            out_specs=out_specs,
            compiler_params=_params("parallel", "parallel", "arbitrary"),
            name=f"attn_a_prompt_g{gi}",
        )(*args)
        o_state = outs[0].reshape(m_rows, gw)
        if not final:
            lse_state = outs[1].reshape(m_rows, LANES)
    return o_state


def _attn_a_sample_kernel(q_ref, kvn_ref, c1_ref, c2_ref, c3_ref, o_ref, *, t_len, scale):
    gw = A_HG * HD_A
    cache_refs = (c1_ref, c2_ref, c3_ref)
    sub = lax.broadcasted_iota(jnp.int32, (A_HG, gw), 0)
    head_of_lane = lax.broadcasted_iota(jnp.int32, (A_HG, gw), 1) // HD_A
    diag = sub == head_of_lane
    for t in range(t_len):
        qrow = q_ref[t]
        o_mix = None
        lse_mix = None
        for gi, (win, dil) in enumerate(A_CONFIGS):
            cref = cache_refs[gi]
            lb = cref.shape[0] * dil
            res = (lb + t) % dil
            qg = qrow[:, gi * gw:(gi + 1) * gw].astype(F32)
            qblk = jnp.where(diag, jnp.broadcast_to(qg, (A_HG, gw)), 0.0)
            kc = cref[:, res * 2 * gw:res * 2 * gw + gw].astype(BF16)
            vc = cref[:, res * 2 * gw + gw:(res + 1) * 2 * gw].astype(BF16)
            s_c = lax.dot_general(qblk.astype(BF16), kc, (((1,), (1,)), ((), ())),
                                  preferred_element_type=F32) * scale
            ci = lax.broadcasted_iota(jnp.int32, s_c.shape, 1) * dil + res
            dist = lb + t - ci
            s_c = jnp.where((dist >= 0) & (dist <= win), s_c, NEG_INF)
            new_us = [u for u in range(t + 1) if (t - u) % dil == 0 and (t - u) <= win]
            s_n = []
            for u in new_us:
                kn = kvn_ref[u][:, gi * gw:(gi + 1) * gw].astype(BF16).astype(F32)
                s_n.append(jnp.sum(qblk * kn, axis=1, keepdims=True) * scale)
            m = jnp.max(s_c, axis=1, keepdims=True)
            for sn in s_n:
                m = jnp.maximum(m, sn)
            p_c = jnp.exp(s_c - m)
            l = jnp.sum(p_c, axis=1, keepdims=True)
            o = jnp.dot(p_c.astype(BF16), vc, preferred_element_type=F32)
            for u, sn in zip(new_us, s_n):
                p_n = jnp.exp(sn - m)
                l = l + p_n
                vn = kvn_ref[u][:, (A_GROUPS + gi) * gw:(A_GROUPS + gi + 1) * gw].astype(BF16).astype(F32)
                o = o + p_n.astype(BF16).astype(F32) * vn
            o = o / l
            lse = m + jnp.log(l)
            if o_mix is None:
                o_mix, lse_mix = o, lse
            else:
                top = jnp.maximum(lse_mix, lse)
                lse_new = top + jnp.log(jnp.exp(lse_mix - top) + jnp.exp(lse - top))
                o_mix = jnp.exp(lse_mix - lse_new) * o_mix + jnp.exp(lse - lse_new) * o
                lse_mix = lse_new
        o_ref[t] = jnp.sum(jnp.where(diag, o_mix, 0.0), axis=0, keepdims=True)


def attn_a_sample(q, kv_new, caches, *, t_len, nb):
    gw = A_HG * HD_A
    cache_args = []
    cache_specs = []
    for (win, dil), c in zip(A_CONFIGS, caches):
        lb = c.shape[1]
        assert lb % dil == 0 and (dil == 1 or t_len <= dil)
        nres = min(dil, t_len)
        cache_args.append(c.reshape(nb, lb // dil, dil * 2 * gw))
        cache_specs.append(pl.BlockSpec((None, lb // dil, nres * 2 * gw), lambda b: (b, 0, 0)))
    out = pl.pallas_call(
        functools.partial(_attn_a_sample_kernel, t_len=t_len, scale=HD_A ** -0.5),
        out_shape=jax.ShapeDtypeStruct((t_len, nb, 1, gw), F32),
        grid=(nb,),
        in_specs=[pl.BlockSpec((t_len, None, 1, A_GROUPS * gw), lambda b: (0, b, 0, 0)),
                  pl.BlockSpec((t_len, None, 1, 2 * A_GROUPS * gw), lambda b: (0, b, 0, 0))] + cache_specs,
        out_specs=pl.BlockSpec((t_len, None, 1, gw), lambda b: (0, b, 0, 0)),
        compiler_params=_params("parallel"),
        name="attn_a_sample",
    )(q.reshape(t_len, nb, 1, A_GROUPS * gw), kv_new.reshape(t_len, nb, 1, 2 * A_GROUPS * gw), *cache_args)
    return out.reshape(t_len * nb, gw)


def _attn_b_kernel(q_ref, kvc_ref, kvp_ref, sink_ref, o_ref, *, scale, first_block_has_prev):
    mb = pl.program_id(1)
    tq = q_ref.shape[0]
    kw = KV_B * HD_B
    kv = jnp.concatenate([kvp_ref[...], kvc_ref[...]], axis=0)
    nk = kv.shape[0]
    qi = lax.broadcasted_iota(jnp.int32, (tq, nk), 0)
    kj = lax.broadcasted_iota(jnp.int32, (tq, nk), 1)
    dist = qi + BLOCK - kj
    valid = (dist >= 0) & (dist <= WINDOW_B)
    if not first_block_has_prev:
        valid = valid & ((mb > 0) | (kj >= BLOCK))
    lane = lax.broadcasted_iota(jnp.int32, (nk, LANES), 1)
    lane_q = lax.broadcasted_iota(jnp.int32, (tq, LANES), 1)
    group = (LANES // HD_B) * (q_ref.shape[1] // LANES) // KV_B
    pairs = group * HD_B // LANES
    for kvh in range(KV_B):
        pc = (kvh * HD_B // LANES) * LANES
        kpair = kv[:, pc:pc + LANES]
        vpair = kv[:, kw + pc:kw + pc + LANES]
        if (kvh * HD_B) % LANES == 0:
            k_lo = jnp.where(lane < HD_B, kpair, 0.0)
            v_lo = jnp.where(lane < HD_B, vpair, 0.0)
            k_hi = pltpu.roll(k_lo, HD_B, 1)
            v_hi = pltpu.roll(v_lo, HD_B, 1)
        else:
            k_hi = jnp.where(lane >= HD_B, kpair, 0.0)
            v_hi = jnp.where(lane >= HD_B, vpair, 0.0)
            k_lo = pltpu.roll(k_hi, HD_B, 1)
            v_lo = pltpu.roll(v_hi, HD_B, 1)
        k2 = jnp.concatenate([k_lo, k_hi], axis=0).astype(BF16)
        v2 = jnp.concatenate([v_lo, v_hi], axis=0).astype(BF16)
        for pr in range(pairs):
            col = (kvh * pairs + pr) * LANES
            q = q_ref[:, col:col + LANES]
            s = lax.dot_general(q, k2, (((1,), (1,)), ((), ())), preferred_element_type=F32) * scale
            ps = []
            inv = []
            for half in range(2):
                sink = sink_ref[(kvh * pairs + pr) * 2 + half]
                sh = jnp.where(valid, s[:, half * nk:(half + 1) * nk], NEG_INF)
                m = jnp.maximum(jnp.max(sh, axis=1, keepdims=True), sink)
                p = jnp.exp(sh - m)
                l = jnp.sum(p, axis=1, keepdims=True) + jnp.exp(sink - m)
                ps.append(p)
                inv.append(1.0 / l)
            p2 = jnp.concatenate(ps, axis=1).astype(BF16)
            o = jnp.dot(p2, v2, preferred_element_type=F32)
            o_ref[:, col:col + LANES] = (o * jnp.where(lane_q < HD_B, inv[0], inv[1])).astype(o_ref.dtype)


def attn_b(q, kv_cur, kv_prev, sinks, *, nb, nblk, tq, first_block_has_prev):
    dq = q.shape[2]
    dkv = kv_cur.shape[2]
    if kv_prev is None:
        prev_arg = kv_cur
        prev_spec = pl.BlockSpec((None, BLOCK, dkv), lambda b, mb: (b, jnp.maximum(mb - 1, 0), 0))
    else:
        assert nblk == 1
        prev_arg = kv_prev
        prev_spec = pl.BlockSpec((None, BLOCK, dkv), lambda b, mb: (b, 0, 0))
    return pl.pallas_call(
        functools.partial(_attn_b_kernel, scale=HD_B ** -0.5, first_block_has_prev=first_block_has_prev),
        out_shape=jax.ShapeDtypeStruct(q.shape, BF16),
        grid=(nb, nblk),
        in_specs=[pl.BlockSpec((None, tq, dq), lambda b, mb: (b, mb, 0)),
                  pl.BlockSpec((None, BLOCK, dkv), lambda b, mb: (b, mb, 0)),
                  prev_spec,
                  pl.BlockSpec(memory_space=pltpu.SMEM)],
        out_specs=pl.BlockSpec((None, tq, dq), lambda b, mb: (b, mb, 0)),
        compiler_params=_params("parallel", "arbitrary"),
        name="attn_b",
    )(q, kv_cur, prev_arg, sinks)


def _pad_cols(w, n):
    return jnp.pad(w, ((0, 0), (0, n - w.shape[1])))


def kernel(x_prompt, x_sample, cache_a1, cache_a2, cache_a3, cache_b, state_conv, norm_attn, norm_ffn,
           w_qkv_a, w_o_a, norm_kv, w_kv_b, w_q_b, sinks_b, w_o_b, w_gate, w_up, conv_w, conv_b, w_down,
           norm_final):
    nbp, seq, d = x_prompt.shape
    nbs, t_len, _ = x_sample.shape
    depth = norm_attn.shape[0]
    n_a = w_qkv_a.shape[0]
    past_len = PAST_LEN
    d_ff = w_gate.shape[2]
    f_pad = -(-d_ff // FF_ALIGN) * FF_ALIGN
    gw = A_HG * HD_A
    mp = nbp * seq
    ms = nbs * t_len
    tm_p = 1024
    tiles_per_seq = seq // tm_p
    a_caches = (cache_a1, cache_a2, cache_a3)

    pos_p = jnp.arange(seq, dtype=jnp.int32)
    pos_s = jnp.repeat(past_len + jnp.arange(t_len, dtype=jnp.int32), nbs)
    tabs_a_p, tabs_a_s = rope_tables(pos_p, HD_A), rope_tables(pos_s, HD_A)
    tabs_b_p, tabs_b_s = rope_tables(pos_p, HD_B), rope_tables(pos_s, HD_B)

    hp = x_prompt.reshape(mp, d)
    hs = x_sample.transpose(1, 0, 2).reshape(ms, d)

    new_a_p = [[] for _ in range(A_GROUPS)]
    new_a_s = [[] for _ in range(A_GROUPS)]
    conv_p, conv_s = [], []
    kv_p = kv_s = None

    for layer in range(depth):
        if layer < n_a:
            wqkv = w_qkv_a[layer].astype(BF16)
            wo = w_o_a[layer].astype(BF16)
            xp, = rmsnorm(hp, norm_attn[layer][None], BF16, 512)
            xs, = rmsnorm(hs, norm_attn[layer][None], BF16, ms)
            nq = A_GROUPS * gw
            tn = 512
            qkv_p, kvf_p = proj(xp, wqkv, tabs_a_p, col0=0, ncols=3 * nq, tm=tm_p, tn=tn,
                                pos_tiles=tiles_per_seq, n_rope=2 * nq // tn, f32_from=nq // tn, shift=HD_A // ROT_FRAC // 2)
            qkv_s, kvf_s = proj(xs, wqkv, tabs_a_s, col0=0, ncols=3 * nq, tm=ms, tn=tn,
                                pos_tiles=1, n_rope=2 * nq // tn, f32_from=nq // tn, shift=HD_A // ROT_FRAC // 2)
            ap = attn_a_prompt(qkv_p, nb=nbp, seq=seq)
            as_ = attn_a_sample(qkv_s[:, :nq], kvf_s, [c[layer] for c in a_caches], t_len=t_len, nb=nbs).astype(BF16)
            for gi, (win, dil) in enumerate(A_CONFIGS):
                k_p = kvf_p[:, gi * gw:(gi + 1) * gw].reshape(nbp, seq, A_HG, HD_A)
                v_p = kvf_p[:, nq + gi * gw:nq + (gi + 1) * gw].reshape(nbp, seq, A_HG, HD_A)
                new_a_p[gi].append(jnp.stack([k_p, v_p], axis=2)[:, -min(win, seq):])
                k_s = kvf_s[:, gi * gw:(gi + 1) * gw].reshape(t_len, nbs, A_HG, HD_A)
                v_s = kvf_s[:, nq + gi * gw:nq + (gi + 1) * gw].reshape(t_len, nbs, A_HG, HD_A)
                kv_new = jnp.stack([k_s, v_s], axis=2).transpose(1, 0, 2, 3, 4)
                new_a_s[gi].append(jnp.concatenate([a_caches[gi][layer], kv_new], axis=1)[:, t_len:])
        else:
            lb = layer - n_a
            wq = w_q_b[lb].astype(BF16)
            wo = w_o_b[lb].astype(BF16)
            kw = KV_B * HD_B
            if layer == n_a:
                wkv = w_kv_b.astype(BF16)
                gains = jnp.stack([norm_attn[layer], norm_kv])
                xp, xkv_p = rmsnorm(hp, gains, BF16, 512)
                xs, xkv_s = rmsnorm(hs, gains, BF16, ms)
                _, kv_p = proj(xkv_p, wkv, tabs_b_p, col0=0, ncols=2 * kw, tm=tm_p, tn=kw,
                               pos_tiles=tiles_per_seq, n_rope=1, f32_from=0, shift=HD_B // ROT_FRAC // 2)
                _, kv_s = proj(xkv_s, wkv, tabs_b_s, col0=0, ncols=2 * kw, tm=ms, tn=kw,
                               pos_tiles=1, n_rope=1, f32_from=0, shift=HD_B // ROT_FRAC // 2)
            else:
                xp, = rmsnorm(hp, norm_attn[layer][None], BF16, 512)
                xs, = rmsnorm(hs, norm_attn[layer][None], BF16, ms)
            dq = wq.shape[1]
            q_p = proj(xp, wq, tabs_b_p, col0=0, ncols=dq, tm=tm_p, tn=512, pos_tiles=tiles_per_seq,
                       n_rope=dq // 512, f32_from=None, shift=HD_B // ROT_FRAC // 2)
            q_s = proj(xs, wq, tabs_b_s, col0=0, ncols=dq, tm=ms, tn=512, pos_tiles=1,
                       n_rope=dq // 512, f32_from=None, shift=HD_B // ROT_FRAC // 2)
            ap = attn_b(q_p.reshape(nbp, seq, dq), kv_p.reshape(nbp, seq, 2 * kw), None, sinks_b[lb],
                        nb=nbp, nblk=seq // BLOCK, tq=BLOCK, first_block_has_prev=False).reshape(mp, dq)
            q_sb = jnp.pad(q_s.reshape(t_len, nbs, dq).transpose(1, 0, 2), ((0, 0), (0, 8 - t_len), (0, 0)))
            kv_sb = jnp.pad(kv_s.reshape(t_len, nbs, 2 * kw).transpose(1, 0, 2), ((0, 0), (0, BLOCK - t_len), (0, 0)))
            o_sb = attn_b(q_sb, kv_sb, cache_b.reshape(nbs, WINDOW_B, 2 * kw), sinks_b[lb],
                          nb=nbs, nblk=1, tq=8, first_block_has_prev=True)
            as_ = o_sb[:, :t_len].transpose(1, 0, 2).reshape(ms, dq)
        hp = mm_res(ap, wo, hp, tm=1024, tn=1024, tk=wo.shape[0])
        hs = mm_res(as_, wo, hs, tm=ms, tn=1024, tk=wo.shape[0])

        wg = _pad_cols(w_gate[layer], f_pad).astype(BF16)
        wu = _pad_cols(w_up[layer], f_pad).astype(BF16)
        wd = jnp.pad(w_down[layer], ((0, f_pad - d_ff), (0, 0))).astype(BF16)
        cw = _pad_cols(conv_w[layer], f_pad)
        cb = _pad_cols(conv_b[layer][None], f_pad)
        xp, = rmsnorm(hp, norm_ffn[layer][None], BF16, 512)
        xs, = rmsnorm(hs, norm_ffn[layer][None], BF16, ms)
        act_p, tails = ffn_up_prompt(xp, wg, wu, cw, cb, tm=tm_p, tf=512, tiles_per_seq=tiles_per_seq)
        prev_s = _pad_cols(state_conv[layer].transpose(1, 0, 2).reshape((CONV_W - 1) * nbs, d_ff), f_pad)
        act_s, tail_s = ffn_up_sample(xs, wg, wu, cw, cb, prev_s, tf=1024, nb=nbs)
        conv_p.append(tails.reshape(nbp, tiles_per_seq, 8, f_pad)[:, -1, 8 - (CONV_W - 1):, :d_ff])
        conv_s.append(tail_s.reshape(CONV_W - 1, nbs, f_pad).transpose(1, 0, 2)[:, :, :d_ff])
        hp = mm_res(act_p, wd, hp, tm=1024, tn=1024, tk=1024)
        hs = mm_res(act_s, wd, hs, tm=ms, tn=1024, tk=1024)

    y_p, = rmsnorm(hp, norm_final[None], F32, 512)
    y_s, = rmsnorm(hs, norm_final[None], F32, ms)
    y_prompt = y_p.reshape(nbp, seq, d)
    y_sample = y_s.reshape(t_len, nbs, d).transpose(1, 0, 2)

    kw = KV_B * HD_B
    kv_p5 = kv_p.reshape(nbp, seq, 2, KV_B, HD_B)
    new_b_p = kv_p5[:, -min(WINDOW_B, seq):]
    kv_s5 = kv_s.reshape(t_len, nbs, 2, KV_B, HD_B).transpose(1, 0, 2, 3, 4)
    new_b_s = jnp.concatenate([cache_b, kv_s5], axis=1)[:, t_len:]
    outs = [y_prompt, y_sample]
    for gi in range(A_GROUPS):
        outs += [jnp.stack(new_a_p[gi], axis=0), jnp.stack(new_a_s[gi], axis=0)]
    outs += [new_b_p, new_b_s, jnp.stack(conv_p, axis=0), jnp.stack(conv_s, axis=0)]
    return tuple(outs)
```

```python
import functools

import jax
import jax.numpy as jnp
from jax import lax
from jax.experimental import pallas as pl
from jax.experimental.pallas import tpu as pltpu

F32 = jnp.float32
BF16 = jnp.bfloat16

HD_A = 128
A_HG = 16
A_CONFIGS = ((128, 1), (512, 4), (2048, 16))
A_GROUPS = len(A_CONFIGS)
A_HC = 4
HD_B = 64
KV_B = 8
WINDOW_B = 128
PAST_LEN = 16384
CONV_W = 3
ROPE_THETA = 500000.0
ROT_FRAC = 4
NORM_EPS = 1e-5
BLOCK = 128
NEG_INF = -1e30
LANES = 128
FF_ALIGN = 1024
VMEM_LIMIT = 56 * 1024 * 1024


def _params(*sem):
    return pltpu.CompilerParams(dimension_semantics=sem, vmem_limit_bytes=VMEM_LIMIT)


def _div_pow2(x, n):
    assert n & (n - 1) == 0
    return x >> (n.bit_length() - 1)


def _mod_pow2(x, n):
    assert n & (n - 1) == 0
    return x & (n - 1)


def _rmsnorm_kernel(h_ref, g_ref, *o_refs):
    x = h_ref[...]
    y = x * lax.rsqrt(jnp.mean(x * x, axis=-1, keepdims=True) + NORM_EPS)
    for n, o_ref in enumerate(o_refs):
        o_ref[...] = (y * g_ref[n:n + 1, :]).astype(o_ref.dtype)


def rmsnorm(h, gains, out_dtype, tm):
    m, d = h.shape
    n = gains.shape[0]
    return pl.pallas_call(
        _rmsnorm_kernel,
        out_shape=[jax.ShapeDtypeStruct((m, d), out_dtype)] * n,
        grid=(m // tm,),
        in_specs=[pl.BlockSpec((tm, d), lambda i: (i, 0)),
                  pl.BlockSpec((n, d), lambda i: (0, 0))],
        out_specs=[pl.BlockSpec((tm, d), lambda i: (i, 0))] * n,
        compiler_params=_params("parallel"),
        name="rmsnorm",
    )(h, gains)


def rope_tables(pos, hd):
    rot = hd // ROT_FRAC
    half = rot // 2
    inv = ROPE_THETA ** (-jnp.arange(half, dtype=F32) * 2.0 / rot)
    ang = pos.astype(F32)[:, None] * inv[None, :]
    cos, sin = jnp.cos(ang), jnp.sin(ang)
    s = pos.shape[0]
    ones = jnp.ones((s, hd - rot), F32)
    zrest = jnp.zeros((s, hd - rot), F32)
    zhalf = jnp.zeros((s, half), F32)
    reps = LANES // hd
    cos_t = jnp.tile(jnp.concatenate([cos, cos, ones], axis=1), (1, reps))
    sa_t = jnp.tile(jnp.concatenate([-sin, zhalf, zrest], axis=1), (1, reps))
    sb_t = jnp.tile(jnp.concatenate([zhalf, sin, zrest], axis=1), (1, reps))
    return cos_t, sa_t, sb_t


def _proj_kernel(x_ref, w_ref, cos_ref, sa_ref, sb_ref, *refs, nj, n_rope, f32_from, shift, dil, has_bf, has_f32):
    refs = list(refs)
    obf_ref = refs.pop(0) if has_bf else None
    of32_ref = refs.pop(0) if has_f32 else None
    slab_ref = refs.pop(0) if dil > 1 else None
    j = pl.program_id(1)
    acc = jnp.dot(x_ref[...], w_ref[...], preferred_element_type=F32)
    tm, tn = acc.shape

    def write(roped, keep_f32):
        for c in range(tn // LANES):
            sl = slice(c * LANES, (c + 1) * LANES)
            xh = acc[:, sl]
            if roped:
                xh = (xh * cos_ref[...] + pltpu.roll(xh, LANES - shift, 1) * sa_ref[...]
                      + pltpu.roll(xh, shift, 1) * sb_ref[...])
            if keep_f32:
                of32_ref[:, sl] = xh
            if not has_bf:
                continue
            if dil == 1:
                obf_ref[:, sl] = xh.astype(obf_ref.dtype)
            else:
                slab_ref[...] = xh
                for r in range(dil):
                    obf_ref[r, :, sl] = slab_ref[pl.ds(r, tm // dil, stride=dil), :].astype(obf_ref.dtype)

    for roped in (True, False):
        for keep_f32 in (True, False):
            if keep_f32 and not has_f32:
                continue
            lo = 0 if roped else n_rope
            hi = n_rope if roped else nj
            if keep_f32:
                lo = max(lo, f32_from)
            elif has_f32:
                hi = min(hi, f32_from)
            if hi <= lo:
                continue
            pl.when((j >= lo) & (j < hi))(functools.partial(write, roped, keep_f32))


def proj(xn, w, layer, tables, *, segs, seg_cols, tm, tn, tiles_per_seq, n_rope_segs, f32_from_seg, shift,
         dil=1, has_bf=True):
    m, k = xn.shape
    per_seg = seg_cols // tn
    nj = len(segs) * per_seg
    ncols = len(segs) * seg_cols
    starts = [s // tn for s in segs]
    step = starts[1] - starts[0] if len(segs) > 1 else 0
    assert all(starts[a] == starts[0] + a * step for a in range(len(segs)))
    has_f32 = f32_from_seg is not None
    f32_from = f32_from_seg * per_seg if has_f32 else nj
    out_shape, out_specs = [], []
    if has_bf:
        if dil == 1:
            out_shape.append(jax.ShapeDtypeStruct((m, ncols), BF16))
            out_specs.append(pl.BlockSpec((tm, tn), lambda i, j: (i, j)))
        else:
            nseq = m // (tm * tiles_per_seq)
            out_shape.append(jax.ShapeDtypeStruct((nseq, dil, tm * tiles_per_seq // dil, ncols), BF16))
            out_specs.append(pl.BlockSpec((None, dil, tm // dil, tn),
                                          lambda i, j: (i // tiles_per_seq, 0, i % tiles_per_seq, j)))
    if has_f32:
        out_shape.append(jax.ShapeDtypeStruct((m, ncols - f32_from * tn), F32))
        out_specs.append(pl.BlockSpec((tm, tn), lambda i, j: (i, jnp.maximum(j - f32_from, 0))))
    tab_spec = pl.BlockSpec((tm, LANES), lambda i, j: (i % tiles_per_seq, 0))
    outs = pl.pallas_call(
        functools.partial(_proj_kernel, nj=nj, n_rope=n_rope_segs * per_seg, f32_from=f32_from, shift=shift,
                          dil=dil, has_bf=has_bf, has_f32=has_f32),
        out_shape=out_shape,
        grid=(m // tm, nj),
        in_specs=[pl.BlockSpec((tm, k), lambda i, j: (i, 0)),
                  pl.BlockSpec((None, k, tn),
                               lambda i, j: (layer, 0, starts[0] + (j // per_seg) * step + j % per_seg)),
                  tab_spec, tab_spec, tab_spec],
        out_specs=out_specs,
        scratch_shapes=[pltpu.VMEM((tm, LANES), F32)] if dil > 1 else [],
        compiler_params=_params("arbitrary", "arbitrary"),
        name="proj",
    )(xn, w, *tables)
    return outs if len(outs) > 1 else outs[0]


def _mm_res_kernel(x_ref, w_ref, res_ref, o_ref):
    @pl.when(pl.program_id(2) == 0)
    def _():
        o_ref[...] = res_ref[...]

    o_ref[...] += jnp.dot(x_ref[...], w_ref[...], preferred_element_type=F32)


def mm_res(x, w, layer, res, *, tm, tn, tk):
    m, kdim = x.shape
    n = w.shape[2]
    return pl.pallas_call(
        _mm_res_kernel,
        out_shape=jax.ShapeDtypeStruct((m, n), F32),
        grid=(m // tm, n // tn, kdim // tk),
        in_specs=[pl.BlockSpec((tm, tk), lambda i, j, k: (i, k)),
                  pl.BlockSpec((None, tk, tn), lambda i, j, k: (layer, k, j)),
                  pl.BlockSpec((tm, tn), lambda i, j, k: (i, j))],
        out_specs=pl.BlockSpec((tm, tn), lambda i, j, k: (i, j)),
        compiler_params=_params("parallel", "parallel", "arbitrary"),
        name="mm_res",
    )(x, w, res)


def _silu(c):
    return c * jax.nn.sigmoid(c)


def _shift_copies(refs, n_shift, sem):
    caches, news, outs = refs[:n_shift], refs[n_shift:2 * n_shift], refs[2 * n_shift:]
    copies = []
    for a, (c_ref, n_ref, o_ref) in enumerate(zip(caches, news, outs)):
        rows, t = c_ref.shape[1], n_ref.shape[1]
        copies.append(pltpu.make_async_copy(c_ref.at[:, pl.ds(t, rows - t)], o_ref.at[:, pl.ds(0, rows - t)],
                                            sem.at[2 * a]))
        copies.append(pltpu.make_async_copy(n_ref, o_ref.at[:, pl.ds(rows - t, t)], sem.at[2 * a + 1]))
    return copies


def _ffn_up_prompt_kernel(*refs, tiles_per_seq, n_shift):
    x_ref, wg_ref, wu_ref, cw_ref, cb_ref = refs[:5]
    shift_in = refs[5:5 + 2 * n_shift]
    act_ref, tail_ref = refs[5 + 2 * n_shift:7 + 2 * n_shift]
    shift_out = refs[7 + 2 * n_shift:7 + 3 * n_shift]
    carry_ref = refs[7 + 3 * n_shift]
    i = pl.program_id(0)
    j = pl.program_id(1)
    if n_shift:
        copies = _shift_copies(list(shift_in) + list(shift_out), n_shift, refs[8 + 3 * n_shift])

        @pl.when((i == 0) & (j == 0))
        def _():
            for cp in copies:
                cp.start()

    x = x_ref[...]
    g = jnp.dot(x, wg_ref[...], preferred_element_type=F32)
    u = jnp.dot(x, wu_ref[...], preferred_element_type=F32)
    tm = g.shape[0]

    @pl.when((i % tiles_per_seq) == 0)
    def _():
        carry_ref[j] = jnp.zeros(carry_ref.shape[1:], F32)

    prev = carry_ref[j]
    last = g[tm - 8:, :]
    carry_ref[j] = last
    tail_ref[...] = last
    row = lax.broadcasted_iota(jnp.int32, (tm, 1), 0)
    g1 = jnp.where(row == 0, prev[7:8, :], pltpu.roll(g, 1, 0))
    g2 = jnp.where(row == 0, prev[6:7, :], jnp.where(row == 1, prev[7:8, :], pltpu.roll(g, 2, 0)))
    c = cb_ref[...] + cw_ref[0:1, :] * g2 + cw_ref[1:2, :] * g1 + cw_ref[2:3, :] * g
    act_ref[...] = (_silu(c) * u).astype(act_ref.dtype)

    if n_shift:
        @pl.when((i == pl.num_programs(0) - 1) & (j == pl.num_programs(1) - 1))
        def _():
            for cp in copies:
                cp.wait()


def ffn_up_prompt(xn, wg, wu, cw, cb, layer, shifts, *, tm, tf, tiles_per_seq):
    m, d = xn.shape
    f = wg.shape[2]
    nj = f // tf
    n_shift = len(shifts)
    any_spec = pl.BlockSpec(memory_space=pl.ANY)
    outs = pl.pallas_call(
        functools.partial(_ffn_up_prompt_kernel, tiles_per_seq=tiles_per_seq, n_shift=n_shift),
        out_shape=[jax.ShapeDtypeStruct((m, f), BF16),
                   jax.ShapeDtypeStruct((m // tm, 8, f), F32)]
        + [jax.ShapeDtypeStruct(c.shape, c.dtype) for c, _ in shifts],
        grid=(m // tm, nj),
        in_specs=[pl.BlockSpec((tm, d), lambda i, j: (i, 0)),
                  pl.BlockSpec((None, d, tf), lambda i, j: (layer, 0, j)),
                  pl.BlockSpec((None, d, tf), lambda i, j: (layer, 0, j)),
                  pl.BlockSpec((None, CONV_W, tf), lambda i, j: (layer, 0, j)),
                  pl.BlockSpec((None, 1, tf), lambda i, j: (layer, 0, j))] + [any_spec] * (2 * n_shift),
        out_specs=[pl.BlockSpec((tm, tf), lambda i, j: (i, j)),
                   pl.BlockSpec((None, 8, tf), lambda i, j: (i, 0, j))] + [any_spec] * n_shift,
        scratch_shapes=[pltpu.VMEM((nj, 8, tf), F32)]
        + ([pltpu.SemaphoreType.DMA((2 * n_shift,))] if n_shift else []),
        compiler_params=_params("arbitrary", "arbitrary"),
        name="ffn_up_prompt",
    )(xn, wg, wu, cw, cb, *[c for c, _ in shifts], *[n for _, n in shifts])
    return outs[0], outs[1], list(outs[2:])


def _ffn_up_sample_kernel(x_ref, wg_ref, wu_ref, cw_ref, cb_ref, prev_ref, act_ref, tail_ref, *, nb):
    x = x_ref[...]
    g = jnp.dot(x, wg_ref[...], preferred_element_type=F32)
    u = jnp.dot(x, wu_ref[...], preferred_element_type=F32)
    m = g.shape[0]
    gext = jnp.concatenate([prev_ref[...], g], axis=0)
    c = (cb_ref[...] + cw_ref[0:1, :] * gext[0:m] + cw_ref[1:2, :] * gext[nb:nb + m]
         + cw_ref[2:3, :] * gext[2 * nb:2 * nb + m])
    act_ref[...] = (_silu(c) * u).astype(act_ref.dtype)
    tail_ref[...] = gext[m:, :]


def ffn_up_sample(xn, wg, wu, cw, cb, layer, prev, *, tf, nb):
    m, d = xn.shape
    f = wg.shape[2]
    return pl.pallas_call(
        functools.partial(_ffn_up_sample_kernel, nb=nb),
        out_shape=[jax.ShapeDtypeStruct((m, f), BF16),
                   jax.ShapeDtypeStruct((2 * nb, f), F32)],
        grid=(f // tf,),
        in_specs=[pl.BlockSpec((m, d), lambda j: (0, 0)),
                  pl.BlockSpec((None, d, tf), lambda j: (layer, 0, j)),
                  pl.BlockSpec((None, d, tf), lambda j: (layer, 0, j)),
                  pl.BlockSpec((None, CONV_W, tf), lambda j: (layer, 0, j)),
                  pl.BlockSpec((None, 1, tf), lambda j: (layer, 0, j)),
                  pl.BlockSpec((2 * nb, tf), lambda j: (0, j))],
        out_specs=[pl.BlockSpec((m, tf), lambda j: (0, j)),
                   pl.BlockSpec((2 * nb, tf), lambda j: (0, j))],
        compiler_params=_params("parallel"),
        name="ffn_up_sample",
    )(xn, wg, wu, cw, cb, prev)


def _attn_a_prompt_kernel(*refs, dil, use_prev, has_in, final, scale):
    it = iter(refs)
    q_ref = next(it)
    kc_ref = next(it)
    kp_ref = next(it) if use_prev else None
    vc_ref = next(it)
    vp_ref = next(it) if use_prev else None
    oin_ref = next(it) if has_in else None
    lin_ref = next(it) if has_in else None
    oout_ref = next(it)
    lout_ref = None if final else next(it)

    mb = pl.program_id(2)
    r = pl.program_id(3)
    rows = pl.ds(r, BLOCK, stride=dil) if dil > 1 else slice(None)
    nkeys = 2 * BLOCK if use_prev else BLOCK
    qi = lax.broadcasted_iota(jnp.int32, (BLOCK, nkeys), 0)
    kj = lax.broadcasted_iota(jnp.int32, (BLOCK, nkeys), 1)
    if use_prev:
        dist = qi + BLOCK - kj
        valid = (dist >= 0) & (dist <= BLOCK) & ((mb > 0) | (kj >= BLOCK))
    else:
        valid = qi >= kj
    lane = lax.broadcasted_iota(jnp.int32, (BLOCK, LANES), 1)
    lse_tile = jnp.zeros((BLOCK, LANES), F32)
    lse_in_tile = lin_ref[rows, :] if has_in else None
    for h in range(A_HC):
        sl = slice(h * HD_A, (h + 1) * HD_A)
        q = q_ref[:, sl]
        if use_prev:
            k = jnp.concatenate([kp_ref[:, sl], kc_ref[:, sl]], axis=0)
            v = jnp.concatenate([vp_ref[:, sl], vc_ref[:, sl]], axis=0)
        else:
            k = kc_ref[:, sl]
            v = vc_ref[:, sl]
        s = lax.dot_general(q, k, (((1,), (1,)), ((), ())), preferred_element_type=F32) * scale
        s = jnp.where(valid, s, NEG_INF)
        m = jnp.max(s, axis=1, keepdims=True)
        p = jnp.exp(s - m)
        l = jnp.sum(p, axis=1, keepdims=True)
        o = jnp.dot(p.astype(BF16), v, preferred_element_type=F32) / l
        lse = m + jnp.log(l)
        if has_in:
            lse_in = lse_in_tile[:, h:h + 1]
            top = jnp.maximum(lse_in, lse)
            lse_new = top + jnp.log(jnp.exp(lse_in - top) + jnp.exp(lse - top))
            o = jnp.exp(lse_in - lse_new) * oin_ref[h, rows, :] + jnp.exp(lse - lse_new) * o
            lse = lse_new
        if final:
            oout_ref[:, sl] = o.astype(oout_ref.dtype)
        else:
            oout_ref[h, rows, :] = o
            lse_tile = jnp.where(lane == h, lse, lse_tile)
    if not final:
        lout_ref[rows, :] = lse_tile


def attn_a_prompt(qkv_groups, *, nb, seq):
    m_rows = nb * seq
    gw = A_HG * HD_A
    cw = A_HC * HD_A
    nhc = A_HG // A_HC
    o_state = lse_state = None
    order = sorted(range(A_GROUPS), key=lambda g: -A_CONFIGS[g][1])
    for step, gi in enumerate(order):
        win, dil = A_CONFIGS[gi]
        assert win // dil == BLOCK and seq % (dil * BLOCK) == 0
        nblk = seq // dil // BLOCK
        use_prev = nblk > 1
        has_in = step > 0
        final = step == A_GROUPS - 1
        assert not final or dil == 1
        qkv = qkv_groups[gi]

        def seg_spec(seg, prev):
            if prev:
                return pl.BlockSpec((None, None, BLOCK, cw),
                                    lambda b, hc, mb, r, seg=seg: (b, r, jnp.maximum(mb - 1, 0), seg * nhc + hc))
            return pl.BlockSpec((None, None, BLOCK, cw), lambda b, hc, mb, r, seg=seg: (b, r, mb, seg * nhc + hc))

        o_spec = pl.BlockSpec((A_HC, BLOCK * dil, LANES), lambda b, hc, mb, r: (hc, b * nblk + mb, 0))
        l_spec = pl.BlockSpec((None, BLOCK * dil, LANES), lambda b, hc, mb, r: (hc, b * nblk + mb, 0))
        args = [qkv, qkv]
        in_specs = [seg_spec(0, False), seg_spec(1, False)]
        if use_prev:
            args.append(qkv)
            in_specs.append(seg_spec(1, True))
        args.append(qkv)
        in_specs.append(seg_spec(2, False))
        if use_prev:
            args.append(qkv)
            in_specs.append(seg_spec(2, True))
        if has_in:
            args += [o_state, lse_state]
            in_specs += [o_spec, l_spec]
        if final:
            out_shape = [jax.ShapeDtypeStruct((m_rows, gw), BF16)]
            out_specs = [pl.BlockSpec((BLOCK, cw), lambda b, hc, mb, r: (b * nblk + mb, hc))]
        else:
            out_shape = [jax.ShapeDtypeStruct((A_HG, m_rows, LANES), F32),
                         jax.ShapeDtypeStruct((nhc, m_rows, LANES), F32)]
            out_specs = [o_spec, l_spec]
        outs = pl.pallas_call(
            functools.partial(_attn_a_prompt_kernel, dil=dil, use_prev=use_prev, has_in=has_in, final=final,
                              scale=HD_A ** -0.5),
            out_shape=out_shape,
            grid=(nb, nhc, nblk, dil),
            in_specs=in_specs,
            out_specs=out_specs,
            compiler_params=_params("parallel", "parallel", "arbitrary", "arbitrary"),
            name=f"attn_a_prompt_g{gi}",
        )(*args)
        if final:
            return outs[0]
        o_state, lse_state = outs


def _attn_a_sample_kernel(q_ref, kn_ref, vn_ref, c1_ref, c2_ref, c3_ref, o_ref, *, t_len, scale):
    nh = A_HG
    cache_refs = (c1_ref, c2_ref, c3_ref)
    o_mix = [None] * t_len
    lse_mix = [None] * t_len
    for gi, (win, dil) in enumerate(A_CONFIGS):
        cref = cache_refs[gi]
        nkey = cref.shape[0]
        lb = nkey * dil
        t_sets = [list(range(t_len))] if dil == 1 else [[t] for t in range(t_len)]
        for ts in t_sets:
            nt = len(ts)
            res = (lb + ts[0]) % dil
            nrow = nt * nh
            qf = q_ref[ts[0]:ts[0] + nt, gi].reshape(nrow, HD_A).astype(BF16)
            kmat = cref[:, res, 0].reshape(nkey * nh, HD_A).astype(BF16)
            vmat = cref[:, res, 1].reshape(nkey * nh, HD_A).astype(BF16)
            s = lax.dot_general(qf, kmat, (((1,), (1,)), ((), ())), preferred_element_type=F32) * scale
            col = lax.broadcasted_iota(jnp.int32, s.shape, 1)
            rowi = lax.broadcasted_iota(jnp.int32, s.shape, 0)
            dist = lb + ts[0] + _div_pow2(rowi, nh) - (_div_pow2(col, nh) * dil + res)
            s = jnp.where((_mod_pow2(col, nh) == _mod_pow2(rowi, nh)) & (dist >= 0) & (dist <= win), s, NEG_INF)
            rowt = ts[0] + _div_pow2(lax.broadcasted_iota(jnp.int32, (nrow, 1), 0), nh)
            qf32 = qf.astype(F32)
            news = []
            for u in range(ts[-1] + 1):
                if not any((t - u) % dil == 0 and 0 <= t - u <= win for t in ts):
                    continue
                kn = jnp.tile(kn_ref[u, gi].astype(BF16).astype(F32), (nt, 1))
                ok = (rowt >= u) & (_mod_pow2(rowt - u, dil) == 0) & (rowt - u <= win)
                sn = jnp.where(ok, jnp.sum(qf32 * kn, axis=1, keepdims=True) * scale, NEG_INF)
                news.append((u, sn))
            m = jnp.max(s, axis=1, keepdims=True)
            for _, sn in news:
                m = jnp.maximum(m, sn)
            p = jnp.exp(s - m)
            l = jnp.sum(p, axis=1, keepdims=True)
            o = jnp.dot(p.astype(BF16), vmat, preferred_element_type=F32)
            for u, sn in news:
                pn = jnp.exp(sn - m)
                l = l + pn
                vn = jnp.tile(vn_ref[u, gi].astype(BF16).astype(F32), (nt, 1))
                o = o + pn.astype(BF16).astype(F32) * vn
            o = o / l
            lse = m + jnp.log(l)
            for a, t in enumerate(ts):
                o_t = o[a * nh:(a + 1) * nh]
                lse_t = lse[a * nh:(a + 1) * nh]
                if o_mix[t] is None:
                    o_mix[t], lse_mix[t] = o_t, lse_t
                else:
                    top = jnp.maximum(lse_mix[t], lse_t)
                    lse_new = top + jnp.log(jnp.exp(lse_mix[t] - top) + jnp.exp(lse_t - top))
                    o_mix[t] = jnp.exp(lse_mix[t] - lse_new) * o_mix[t] + jnp.exp(lse_t - lse_new) * o_t
                    lse_mix[t] = lse_new
    for t in range(t_len):
        o_ref[t] = o_mix[t]


def attn_a_sample(q, k_new, v_new, caches, *, t_len, nb):
    cache_args = []
    cache_specs = []
    for (win, dil), c in zip(A_CONFIGS, caches):
        lb = c.shape[1]
        assert lb % dil == 0 and (dil == 1 or t_len <= dil)
        nres = min(dil, t_len)
        cache_args.append(c.reshape(nb, lb // dil, dil, 2, A_HG, HD_A))
        cache_specs.append(pl.BlockSpec((None, lb // dil, nres, 2, A_HG, HD_A), lambda b: (b, 0, 0, 0, 0, 0)))
    row_spec = pl.BlockSpec((None, t_len, A_GROUPS, A_HG, HD_A), lambda b: (b, 0, 0, 0, 0))
    return pl.pallas_call(
        functools.partial(_attn_a_sample_kernel, t_len=t_len, scale=HD_A ** -0.5),
        out_shape=jax.ShapeDtypeStruct((nb, t_len, A_HG, HD_A), F32),
        grid=(nb,),
        in_specs=[row_spec, row_spec, row_spec] + cache_specs,
        out_specs=pl.BlockSpec((None, t_len, A_HG, HD_A), lambda b: (b, 0, 0, 0)),
        compiler_params=_params("parallel"),
        name="attn_a_sample",
    )(q, k_new, v_new, *cache_args)


def _attn_b_kernel(q_ref, kvc_ref, kvp_ref, sink_ref, o_ref, *, scale, tq, stack, first_block_has_prev):
    mb = pl.program_id(1)
    nrow = stack * tq
    kw = KV_B * HD_B
    kv = jnp.concatenate([kvp_ref[...], kvc_ref[...]], axis=0)
    nk = kv.shape[0]
    rowi = lax.broadcasted_iota(jnp.int32, (nrow, nk), 0)
    kj = lax.broadcasted_iota(jnp.int32, (nrow, nk), 1)
    dist = _mod_pow2(rowi, tq) + BLOCK - kj
    valid = (dist >= 0) & (dist <= WINDOW_B)
    if not first_block_has_prev:
        valid = valid & ((mb > 0) | (kj >= BLOCK))
    lane = lax.broadcasted_iota(jnp.int32, (nk, LANES), 1)
    lane_q = lax.broadcasted_iota(jnp.int32, (nrow, LANES), 1)
    pair_of_row = _div_pow2(lax.broadcasted_iota(jnp.int32, (nrow, 1), 0), tq)
    tiles_per_kvh = q_ref.shape[1] // LANES // KV_B
    for kvh in range(KV_B):
        pc = (kvh * HD_B // LANES) * LANES
        kpair = kv[:, pc:pc + LANES]
        vpair = kv[:, kw + pc:kw + pc + LANES]
        if (kvh * HD_B) % LANES == 0:
            k_lo = jnp.where(lane < HD_B, kpair, 0.0)
            v_lo = jnp.where(lane < HD_B, vpair, 0.0)
            k_hi = pltpu.roll(k_lo, HD_B, 1)
            v_hi = pltpu.roll(v_lo, HD_B, 1)
        else:
            k_hi = jnp.where(lane >= HD_B, kpair, 0.0)
            v_hi = jnp.where(lane >= HD_B, vpair, 0.0)
            k_lo = pltpu.roll(k_hi, HD_B, 1)
            v_lo = pltpu.roll(v_hi, HD_B, 1)
        k2 = jnp.concatenate([k_lo, k_hi], axis=0).astype(BF16)
        v2 = jnp.concatenate([v_lo, v_hi], axis=0).astype(BF16)
        for tile in range(tiles_per_kvh):
            col = (kvh * tiles_per_kvh + tile) * LANES
            q = q_ref[:, col:col + LANES]
            s = lax.dot_general(q, k2, (((1,), (1,)), ((), ())), preferred_element_type=F32) * scale
            ps = []
            inv = []
            for half in range(2):
                head0 = ((kvh * tiles_per_kvh + tile) * stack) * 2 + half
                sink = jnp.full((nrow, 1), sink_ref[head0], F32)
                for pr in range(1, stack):
                    sink = jnp.where(pair_of_row == pr, sink_ref[head0 + 2 * pr], sink)
                sh = jnp.where(valid, s[:, half * nk:(half + 1) * nk], NEG_INF)
                m = jnp.maximum(jnp.max(sh, axis=1, keepdims=True), sink)
                p = jnp.exp(sh - m)
                l = jnp.sum(p, axis=1, keepdims=True) + jnp.exp(sink - m)
                ps.append(p)
                inv.append(1.0 / l)
            p2 = jnp.concatenate(ps, axis=1).astype(BF16)
            o = jnp.dot(p2, v2, preferred_element_type=F32)
            o_ref[:, col:col + LANES] = (o * jnp.where(lane_q < HD_B, inv[0], inv[1])).astype(o_ref.dtype)


def attn_b(q, kv_cur, kv_prev, sinks, *, nb, nblk, tq, stack, first_block_has_prev):
    dq = q.shape[2]
    dkv = kv_cur.shape[2]
    if kv_prev is None:
        prev_arg = kv_cur
        prev_spec = pl.BlockSpec((None, BLOCK, dkv), lambda b, mb: (b, jnp.maximum(mb - 1, 0), 0))
    else:
        assert nblk == 1
        prev_arg = kv_prev
        prev_spec = pl.BlockSpec((None, BLOCK, dkv), lambda b, mb: (b, 0, 0))
    return pl.pallas_call(
        functools.partial(_attn_b_kernel, scale=HD_B ** -0.5, tq=tq, stack=stack,
                          first_block_has_prev=first_block_has_prev),
        out_shape=jax.ShapeDtypeStruct(q.shape, BF16),
        grid=(nb, nblk),
        in_specs=[pl.BlockSpec((None, stack * tq, dq), lambda b, mb: (b, mb, 0)),
                  pl.BlockSpec((None, BLOCK, dkv), lambda b, mb: (b, mb, 0)),
                  prev_spec,
                  pl.BlockSpec(memory_space=pltpu.SMEM)],
        out_specs=pl.BlockSpec((None, stack * tq, dq), lambda b, mb: (b, mb, 0)),
        compiler_params=_params("parallel", "arbitrary"),
        name="attn_b",
    )(q, kv_cur, prev_arg, sinks)


def _cast_pad_kernel(w_ref, o_ref, *, rows, cols):
    x = w_ref[...]
    tr, tc = x.shape
    if rows % tr or cols % tc:
        r = lax.broadcasted_iota(jnp.int32, x.shape, 0) + pl.program_id(1) * tr
        c = lax.broadcasted_iota(jnp.int32, x.shape, 1) + pl.program_id(2) * tc
        x = jnp.where((r < rows) & (c < cols), x, 0.0)
    o_ref[...] = x.astype(o_ref.dtype)


def cast_pad(w, rows_pad, cols_pad, *, tr=512, tc=1024):
    nl, rows, cols = w.shape
    return pl.pallas_call(
        functools.partial(_cast_pad_kernel, rows=rows, cols=cols),
        out_shape=jax.ShapeDtypeStruct((nl, rows_pad, cols_pad), BF16),
        grid=(nl, rows_pad // tr, cols_pad // tc),
        in_specs=[pl.BlockSpec((None, tr, tc), lambda l, i, j: (l, i, j))],
        out_specs=pl.BlockSpec((None, tr, tc), lambda l, i, j: (l, i, j)),
        compiler_params=_params("parallel", "parallel", "parallel"),
        name="cast_pad",
    )(w)


def _pad_last(w, n):
    return jnp.pad(w, [(0, 0)] * (w.ndim - 1) + [(0, n - w.shape[-1])])


def kernel(x_prompt, x_sample, cache_a1, cache_a2, cache_a3, cache_b, state_conv, norm_attn, norm_ffn,
           w_qkv_a, w_o_a, norm_kv, w_kv_b, w_q_b, sinks_b, w_o_b, w_gate, w_up, conv_w, conv_b, w_down,
           norm_final):
    nbp, seq, d = x_prompt.shape
    nbs, t_len, _ = x_sample.shape
    depth = norm_attn.shape[0]
    n_a = w_qkv_a.shape[0]
    d_ff = w_gate.shape[2]
    f_pad = -(-d_ff // FF_ALIGN) * FF_ALIGN
    gw = A_HG * HD_A
    nq = A_GROUPS * gw
    kw = KV_B * HD_B
    mp = nbp * seq
    ms = nbs * t_len
    tm_p = 1024
    tiles_per_seq = seq // tm_p
    a_caches = (cache_a1, cache_a2, cache_a3)
    shift_a = HD_A // ROT_FRAC // 2
    shift_b = HD_B // ROT_FRAC // 2

    pos_p = jnp.arange(seq, dtype=jnp.int32)
    pos_s = jnp.repeat(PAST_LEN + jnp.arange(t_len, dtype=jnp.int32), nbs)
    tabs_a_p, tabs_a_s = rope_tables(pos_p, HD_A), rope_tables(pos_s, HD_A)
    tabs_b_p, tabs_b_s = rope_tables(pos_p, HD_B), rope_tables(pos_s, HD_B)

    hp = x_prompt.reshape(mp, d)
    hs = x_sample.transpose(1, 0, 2).reshape(ms, d)

    dq = w_q_b.shape[2]
    wqkv_a = cast_pad(w_qkv_a, d, 3 * nq)
    wo_a = cast_pad(w_o_a, gw, d)
    wkv_b = cast_pad(w_kv_b[None], d, 2 * kw)
    wq_b = cast_pad(w_q_b, d, dq)
    wo_b = cast_pad(w_o_b, dq, d)
    wg = cast_pad(w_gate, d, f_pad)
    wu = cast_pad(w_up, d, f_pad)
    wd = cast_pad(w_down, f_pad, d)
    cw = _pad_last(conv_w, f_pad)
    cb = _pad_last(conv_b[:, None], f_pad)

    new_a_p = [[] for _ in range(A_GROUPS)]
    new_a_s = [[] for _ in range(A_GROUPS)]
    conv_p, conv_s = [], []
    kv_p = kv_s = new_b_s = None

    for layer in range(depth):
        shifts = []
        if layer < n_a:
            wo, wo_l = wo_a, layer
            xp, = rmsnorm(hp, norm_attn[layer][None], BF16, 512)
            xs, = rmsnorm(hs, norm_attn[layer][None], BF16, ms)
            qkv_groups = []
            for gi, (win, dil) in enumerate(A_CONFIGS):
                qkv_g, kvf_g = proj(xp, wqkv_a, layer, tabs_a_p, segs=[gi * gw, nq + gi * gw, 2 * nq + gi * gw],
                                    seg_cols=gw, tm=tm_p, tn=512, tiles_per_seq=tiles_per_seq,
                                    n_rope_segs=2, f32_from_seg=1, shift=shift_a, dil=dil)
                qkv_groups.append(qkv_g.reshape(nbp, dil, seq // dil, 3 * gw))
                new_a_p[gi].append(kvf_g.reshape(nbp, seq, 2, A_HG, HD_A)[:, -min(win, seq):])
            ap = attn_a_prompt(qkv_groups, nb=nbp, seq=seq)
            qkv_s = proj(xs, wqkv_a, layer, tabs_a_s, segs=[0, nq, 2 * nq], seg_cols=nq, tm=ms, tn=512, tiles_per_seq=1,
                         n_rope_segs=2, f32_from_seg=0, shift=shift_a, has_bf=False)
            qkv_s = qkv_s.reshape(t_len, nbs, 3, A_GROUPS, A_HG, HD_A).transpose(2, 1, 0, 3, 4, 5)
            as_ = attn_a_sample(qkv_s[0], qkv_s[1], qkv_s[2], [c[layer] for c in a_caches], t_len=t_len, nb=nbs)
            as_ = as_.transpose(1, 0, 2, 3).reshape(ms, gw).astype(BF16)
            for gi in range(A_GROUPS):
                kv_new = jnp.stack([qkv_s[1][:, :, gi], qkv_s[2][:, :, gi]], axis=2)
                shifts.append((a_caches[gi][layer], kv_new))
        else:
            lb = layer - n_a
            wo, wo_l = wo_b, lb
            if layer == n_a:
                gains = jnp.stack([norm_attn[layer], norm_kv])
                xp, xkv_p = rmsnorm(hp, gains, BF16, 512)
                xs, xkv_s = rmsnorm(hs, gains, BF16, ms)
                kv_p = proj(xkv_p, wkv_b, 0, tabs_b_p, segs=[0, kw], seg_cols=kw, tm=tm_p, tn=kw,
                            tiles_per_seq=tiles_per_seq, n_rope_segs=1, f32_from_seg=0, shift=shift_b, has_bf=False)
                kv_s = proj(xkv_s, wkv_b, 0, tabs_b_s, segs=[0, kw], seg_cols=kw, tm=ms, tn=kw,
                            tiles_per_seq=1, n_rope_segs=1, f32_from_seg=0, shift=shift_b, has_bf=False)
                kv_s5 = kv_s.reshape(t_len, nbs, 2, KV_B, HD_B).transpose(1, 0, 2, 3, 4)
                shifts.append((cache_b, kv_s5))
            else:
                xp, = rmsnorm(hp, norm_attn[layer][None], BF16, 512)
                xs, = rmsnorm(hs, norm_attn[layer][None], BF16, ms)
            q_p = proj(xp, wq_b, lb, tabs_b_p, segs=[0], seg_cols=dq, tm=tm_p, tn=512, tiles_per_seq=tiles_per_seq,
                       n_rope_segs=1, f32_from_seg=None, shift=shift_b)
            q_s = proj(xs, wq_b, lb, tabs_b_s, segs=[0], seg_cols=dq, tm=ms, tn=512, tiles_per_seq=1,
                       n_rope_segs=1, f32_from_seg=None, shift=shift_b)
            ap = attn_b(q_p.reshape(nbp, seq, dq), kv_p.reshape(nbp, seq, 2 * kw), None, sinks_b[lb],
                        nb=nbp, nblk=seq // BLOCK, tq=BLOCK, stack=1, first_block_has_prev=False).reshape(mp, dq)
            npair = dq // LANES // KV_B
            q_sb = q_s.reshape(t_len, nbs, KV_B, npair, LANES).transpose(1, 3, 0, 2, 4)
            q_sb = q_sb.reshape(nbs, npair * t_len, KV_B * LANES)
            kv_sb = jnp.pad(kv_s.reshape(t_len, nbs, 2 * kw).transpose(1, 0, 2), ((0, 0), (0, BLOCK - t_len), (0, 0)))
            o_sb = attn_b(q_sb, kv_sb, cache_b.reshape(nbs, WINDOW_B, 2 * kw), sinks_b[lb],
                          nb=nbs, nblk=1, tq=t_len, stack=npair, first_block_has_prev=True)
            as_ = o_sb.reshape(nbs, npair, t_len, KV_B, LANES).transpose(2, 0, 3, 1, 4).reshape(ms, dq)
        hp = mm_res(ap, wo, wo_l, hp, tm=1024, tn=1024, tk=wo.shape[1])
        hs = mm_res(as_, wo, wo_l, hs, tm=ms, tn=1024, tk=wo.shape[1])

        xp, = rmsnorm(hp, norm_ffn[layer][None], BF16, 512)
        xs, = rmsnorm(hs, norm_ffn[layer][None], BF16, ms)
        act_p, tails, shifted = ffn_up_prompt(xp, wg, wu, cw, cb, layer, shifts, tm=tm_p, tf=512,
                                              tiles_per_seq=tiles_per_seq)
        if layer < n_a:
            for gi in range(A_GROUPS):
                new_a_s[gi].append(shifted[gi])
        elif layer == n_a:
            new_b_s = shifted[0]
        prev_s = _pad_last(state_conv[layer].transpose(1, 0, 2).reshape((CONV_W - 1) * nbs, d_ff), f_pad)
        act_s, tail_s = ffn_up_sample(xs, wg, wu, cw, cb, layer, prev_s, tf=1024, nb=nbs)
        conv_p.append(tails.reshape(nbp, tiles_per_seq, 8, f_pad)[:, -1, 8 - (CONV_W - 1):, :d_ff])
        conv_s.append(tail_s.reshape(CONV_W - 1, nbs, f_pad).transpose(1, 0, 2)[:, :, :d_ff])
        tk_down = f_pad // 4
        hp = mm_res(act_p, wd, layer, hp, tm=1024, tn=1024, tk=tk_down)
        hs = mm_res(act_s, wd, layer, hs, tm=ms, tn=1024, tk=tk_down)

    y_p, = rmsnorm(hp, norm_final[None], F32, 512)
    y_s, = rmsnorm(hs, norm_final[None], F32, ms)
    y_prompt = y_p.reshape(nbp, seq, d)
    y_sample = y_s.reshape(t_len, nbs, d).transpose(1, 0, 2)

    new_b_p = kv_p.reshape(nbp, seq, 2, KV_B, HD_B)[:, -min(WINDOW_B, seq):]
    outs = [y_prompt, y_sample]
    for gi in range(A_GROUPS):
        outs += [jnp.stack(new_a_p[gi], axis=0), jnp.stack(new_a_s[gi], axis=0)]
    outs += [new_b_p, new_b_s, jnp.stack(conv_p, axis=0), jnp.stack(conv_s, axis=0)]
    return tuple(outs)
```

```python
import functools

import jax
import jax.numpy as jnp
from jax import lax
from jax.experimental import pallas as pl
from jax.experimental.pallas import tpu as pltpu

F32 = jnp.float32
BF16 = jnp.bfloat16

HD_A = 128
A_HG = 16
A_CONFIGS = ((128, 1), (512, 4), (2048, 16))
A_GROUPS = len(A_CONFIGS)
A_LSE_CHUNK = 8
HD_B = 64
KV_B = 8
WINDOW_B = 128
PAST_LEN = 16384
CONV_W = 3
ROPE_THETA = 500000.0
ROT_FRAC = 4
NORM_EPS = 1e-5
BLOCK = 128
NEG_INF = -1e30
LANES = 128
FF_ALIGN = 1024
FFN_ROWS = 64
VMEM_LIMIT = 56 * 1024 * 1024


def _params(*sem):
    return pltpu.CompilerParams(dimension_semantics=sem, vmem_limit_bytes=VMEM_LIMIT)


def _div_pow2(x, n):
    assert n & (n - 1) == 0
    return x >> (n.bit_length() - 1)


def _mod_pow2(x, n):
    assert n & (n - 1) == 0
    return x & (n - 1)


def _rmsnorm_kernel(h_ref, g_ref, *o_refs):
    x = h_ref[...]
    y = x * lax.rsqrt(jnp.mean(x * x, axis=-1, keepdims=True) + NORM_EPS)
    for n, o_ref in enumerate(o_refs):
        o_ref[...] = (y * g_ref[n:n + 1, :]).astype(o_ref.dtype)


def rmsnorm(h, gains, out_dtype, tm):
    m, d = h.shape
    n = gains.shape[0]
    return pl.pallas_call(
        _rmsnorm_kernel,
        out_shape=[jax.ShapeDtypeStruct((m, d), out_dtype)] * n,
        grid=(m // tm,),
        in_specs=[pl.BlockSpec((tm, d), lambda i: (i, 0)),
                  pl.BlockSpec((n, d), lambda i: (0, 0))],
        out_specs=[pl.BlockSpec((tm, d), lambda i: (i, 0))] * n,
        compiler_params=_params("parallel"),
        name="rmsnorm",
    )(h, gains)


def rope_tables(pos, hd):
    rot = hd // ROT_FRAC
    half = rot // 2
    inv = ROPE_THETA ** (-jnp.arange(half, dtype=F32) * 2.0 / rot)
    ang = pos.astype(F32)[:, None] * inv[None, :]
    cos, sin = jnp.cos(ang), jnp.sin(ang)
    s = pos.shape[0]
    ones = jnp.ones((s, hd - rot), F32)
    zrest = jnp.zeros((s, hd - rot), F32)
    zhalf = jnp.zeros((s, half), F32)
    reps = LANES // hd
    cos_t = jnp.tile(jnp.concatenate([cos, cos, ones], axis=1), (1, reps))
    sa_t = jnp.tile(jnp.concatenate([-sin, zhalf, zrest], axis=1), (1, reps))
    sb_t = jnp.tile(jnp.concatenate([zhalf, sin, zrest], axis=1), (1, reps))
    return (jnp.stack([cos_t, jnp.ones_like(cos_t)]), jnp.stack([sa_t, jnp.zeros_like(sa_t)]),
            jnp.stack([sb_t, jnp.zeros_like(sb_t)]))


def _proj_kernel(x_ref, w_ref, cos_ref, sa_ref, sb_ref, *refs, shift, dil, has_bf, has_f32):
    refs = list(refs)
    obf_ref = refs.pop(0) if has_bf else None
    of32_ref = refs.pop(0) if has_f32 else None
    acc_refs = (refs.pop(0), refs.pop(0))
    slab_ref = refs.pop(0) if dil > 1 else None
    s = pl.program_id(0)
    tm, tn = acc_refs[0].shape

    @pl.when(s == 0)
    def _():
        acc_refs[1][...] = jnp.zeros((tm, tn), F32)

    def finish(acc_ref):
        for c in range(tn // LANES):
            sl = slice(c * LANES, (c + 1) * LANES)
            xh = acc_ref[:, sl]
            xh = (xh * cos_ref[...] + pltpu.roll(xh, LANES - shift, 1) * sa_ref[...]
                  + pltpu.roll(xh, shift, 1) * sb_ref[...])
            if has_f32:
                of32_ref[:, sl] = xh
            if not has_bf:
                continue
            if dil == 1:
                obf_ref[:, sl] = xh.astype(obf_ref.dtype)
            else:
                slab_ref[...] = xh
                for r in range(dil):
                    obf_ref[r, :, sl] = slab_ref[pl.ds(r, tm // dil, stride=dil), :].astype(obf_ref.dtype)

    for parity in range(2):
        @pl.when(s % 2 == parity)
        def _(parity=parity):
            acc_refs[parity][...] = jnp.dot(x_ref[...], w_ref[...], preferred_element_type=F32)
            finish(acc_refs[1 - parity])


def proj(xn, w, layer, tables, *, segs, seg_cols, tm, tn, tiles_per_seq, n_rope_segs, f32_from_seg, shift,
         dil=1, has_bf=True):
    m, k = xn.shape
    per_seg = seg_cols // tn
    nj = len(segs) * per_seg
    ncols = len(segs) * seg_cols
    starts = [s // tn for s in segs]
    step = starts[1] - starts[0] if len(segs) > 1 else 0
    assert all(starts[a] == starts[0] + a * step for a in range(len(segs)))
    has_f32 = f32_from_seg is not None
    f32_from = f32_from_seg * per_seg if has_f32 else nj
    n_rope = n_rope_segs * per_seg
    ntiles = (m // tm) * nj

    def cur(s):
        t = jnp.minimum(s, ntiles - 1)
        return t // nj, t % nj

    def prev(s):
        t = jnp.maximum(s - 1, 0)
        return t // nj, t % nj

    def w_map(s):
        _, j = cur(s)
        return layer, 0, starts[0] + (j // per_seg) * step + j % per_seg

    def tab_map(s):
        i, j = prev(s)
        return jnp.where(j < n_rope, 0, 1), i % tiles_per_seq, 0

    def bf_map(s):
        i, j = prev(s)
        return (i, j) if dil == 1 else (i // tiles_per_seq, 0, i % tiles_per_seq, j)

    def f32_map(s):
        i, j = prev(s)
        return i, jnp.maximum(j - f32_from, 0)

    out_shape, out_specs = [], []
    if has_bf:
        if dil == 1:
            out_shape.append(jax.ShapeDtypeStruct((m, ncols), BF16))
            out_specs.append(pl.BlockSpec((tm, tn), bf_map))
        else:
            nseq = m // (tm * tiles_per_seq)
            out_shape.append(jax.ShapeDtypeStruct((nseq, dil, tm * tiles_per_seq // dil, ncols), BF16))
            out_specs.append(pl.BlockSpec((None, dil, tm // dil, tn), bf_map))
    if has_f32:
        out_shape.append(jax.ShapeDtypeStruct((m, ncols - f32_from * tn), F32))
        out_specs.append(pl.BlockSpec((tm, tn), f32_map))
    tab_spec = pl.BlockSpec((None, tm, LANES), tab_map)
    outs = pl.pallas_call(
        functools.partial(_proj_kernel, shift=shift, dil=dil, has_bf=has_bf, has_f32=has_f32),
        out_shape=out_shape,
        grid=(ntiles + 1,),
        in_specs=[pl.BlockSpec((tm, k), lambda s: (cur(s)[0], 0)),
                  pl.BlockSpec((None, k, tn), w_map),
                  tab_spec, tab_spec, tab_spec],
        out_specs=out_specs,
        scratch_shapes=[pltpu.VMEM((tm, tn), F32), pltpu.VMEM((tm, tn), F32)]
        + ([pltpu.VMEM((tm, LANES), F32)] if dil > 1 else []),
        compiler_params=_params("arbitrary"),
        name="proj",
    )(xn, w, *tables)
    return outs if len(outs) > 1 else outs[0]


def _mm_res_kernel(x_ref, w_ref, res_ref, o_ref):
    @pl.when(pl.program_id(2) == 0)
    def _():
        o_ref[...] = res_ref[...]

    o_ref[...] += jnp.dot(x_ref[...], w_ref[...], preferred_element_type=F32)


def mm_res(x, w, layer, res, *, tm, tn, tk):
    m, kdim = x.shape
    n = w.shape[2]
    return pl.pallas_call(
        _mm_res_kernel,
        out_shape=jax.ShapeDtypeStruct((m, n), F32),
        grid=(m // tm, n // tn, kdim // tk),
        in_specs=[pl.BlockSpec((tm, tk), lambda i, j, k: (i, k)),
                  pl.BlockSpec((None, tk, tn), lambda i, j, k: (layer, k, j)),
                  pl.BlockSpec((tm, tn), lambda i, j, k: (i, j))],
        out_specs=pl.BlockSpec((tm, tn), lambda i, j, k: (i, j)),
        compiler_params=_params("parallel", "parallel", "arbitrary"),
        name="mm_res",
    )(x, w, res)


def _silu(c):
    return c * jax.nn.sigmoid(c)


def _ffn_up_prompt_kernel(x_ref, wg_ref, wu_ref, cw_ref, cb_ref, act_ref, tail_ref,
                          g0_ref, u0_ref, g1_ref, u1_ref, carry_ref, *, nj, tiles_per_seq):
    s = pl.program_id(0)
    t_prev = jnp.maximum(s - 1, 0)
    ip = t_prev // nj
    jp = t_prev % nj
    tm, tf = g0_ref.shape

    @pl.when(s == 0)
    def _():
        g1_ref[...] = jnp.zeros((tm, tf), F32)
        u1_ref[...] = jnp.zeros((tm, tf), F32)

    @pl.when((ip % tiles_per_seq) == 0)
    def _():
        carry_ref[jp] = jnp.zeros(carry_ref.shape[1:], F32)

    def finish(g_ref, u_ref):
        for c0 in range(0, tf, LANES):
            sl = slice(c0, c0 + LANES)
            cw0, cw1, cw2, cbv = cw_ref[0:1, sl], cw_ref[1:2, sl], cw_ref[2:3, sl], cb_ref[:, sl]
            prev = carry_ref[jp, :, sl]
            for a in range(0, tm, FFN_ROWS):
                if a == 0:
                    ext = jnp.concatenate([prev, g_ref[0:FFN_ROWS, sl]], axis=0)
                else:
                    ext = g_ref[a - 8:a + FFN_ROWS, sl]
                c = cbv + cw0 * pltpu.roll(ext, 2, 0)[8:] + cw1 * pltpu.roll(ext, 1, 0)[8:] + cw2 * ext[8:]
                act_ref[a:a + FFN_ROWS, sl] = (_silu(c) * u_ref[a:a + FFN_ROWS, sl]).astype(act_ref.dtype)
            last = g_ref[tm - 8:tm, sl]
            carry_ref[jp, :, sl] = last
            tail_ref[:, sl] = last

    bufs = ((g0_ref, u0_ref), (g1_ref, u1_ref))
    for parity in range(2):
        @pl.when(s % 2 == parity)
        def _(parity=parity):
            x = x_ref[...]
            bufs[parity][0][...] = jnp.dot(x, wg_ref[...], preferred_element_type=F32)
            bufs[parity][1][...] = jnp.dot(x, wu_ref[...], preferred_element_type=F32)
            finish(*bufs[1 - parity])


def ffn_up_prompt(xn, wg, wu, cw, cb, layer, *, tm, tf, tiles_per_seq):
    m, d = xn.shape
    f = wg.shape[2]
    nj = f // tf
    ntiles = (m // tm) * nj

    def cur(s):
        t = jnp.minimum(s, ntiles - 1)
        return t // nj, t % nj

    def prev(s):
        t = jnp.maximum(s - 1, 0)
        return t // nj, t % nj

    buf = pltpu.VMEM((tm, tf), F32)
    return pl.pallas_call(
        functools.partial(_ffn_up_prompt_kernel, nj=nj, tiles_per_seq=tiles_per_seq),
        out_shape=[jax.ShapeDtypeStruct((m, f), BF16),
                   jax.ShapeDtypeStruct((m // tm, 8, f), F32)],
        grid=(ntiles + 1,),
        in_specs=[pl.BlockSpec((tm, d), lambda s: (cur(s)[0], 0)),
                  pl.BlockSpec((None, d, tf), lambda s: (layer, 0, cur(s)[1])),
                  pl.BlockSpec((None, d, tf), lambda s: (layer, 0, cur(s)[1])),
                  pl.BlockSpec((None, CONV_W, tf), lambda s: (layer, 0, prev(s)[1])),
                  pl.BlockSpec((None, 1, tf), lambda s: (layer, 0, prev(s)[1]))],
        out_specs=[pl.BlockSpec((tm, tf), lambda s: prev(s)),
                   pl.BlockSpec((None, 8, tf), lambda s: (prev(s)[0], 0, prev(s)[1]))],
        scratch_shapes=[buf, buf, buf, buf, pltpu.VMEM((nj, 8, tf), F32)],
        compiler_params=_params("arbitrary"),
        name="ffn_up_prompt",
    )(xn, wg, wu, cw, cb)


def _ffn_up_sample_kernel(x_ref, wg_ref, wu_ref, cw_ref, cb_ref, prev_ref, act_ref, tail_ref, *, nb):
    x = x_ref[...]
    g = jnp.dot(x, wg_ref[...], preferred_element_type=F32)
    u = jnp.dot(x, wu_ref[...], preferred_element_type=F32)
    m = g.shape[0]
    gext = jnp.concatenate([prev_ref[...], g], axis=0)
    c = (cb_ref[...] + cw_ref[0:1, :] * gext[0:m] + cw_ref[1:2, :] * gext[nb:nb + m]
         + cw_ref[2:3, :] * gext[2 * nb:2 * nb + m])
    act_ref[...] = (_silu(c) * u).astype(act_ref.dtype)
    tail_ref[...] = gext[m:, :]


def ffn_up_sample(xn, wg, wu, cw, cb, layer, prev, *, tf, nb):
    m, d = xn.shape
    f = wg.shape[2]
    return pl.pallas_call(
        functools.partial(_ffn_up_sample_kernel, nb=nb),
        out_shape=[jax.ShapeDtypeStruct((m, f), BF16),
                   jax.ShapeDtypeStruct((2 * nb, f), F32)],
        grid=(f // tf,),
        in_specs=[pl.BlockSpec((m, d), lambda j: (0, 0)),
                  pl.BlockSpec((None, d, tf), lambda j: (layer, 0, j)),
                  pl.BlockSpec((None, d, tf), lambda j: (layer, 0, j)),
                  pl.BlockSpec((None, CONV_W, tf), lambda j: (layer, 0, j)),
                  pl.BlockSpec((None, 1, tf), lambda j: (layer, 0, j)),
                  pl.BlockSpec((2 * nb, tf), lambda j: (0, j))],
        out_specs=[pl.BlockSpec((m, tf), lambda j: (0, j)),
                   pl.BlockSpec((2 * nb, tf), lambda j: (0, j))],
        compiler_params=_params("parallel"),
        name="ffn_up_sample",
    )(xn, wg, wu, cw, cb, prev)


def _attn_a_prompt_kernel(*refs, dil, use_prev, has_in, final, scale):
    it = iter(refs)
    q_ref = next(it)
    kc_ref = next(it)
    kp_ref = next(it) if use_prev else None
    vc_ref = next(it)
    vp_ref = next(it) if use_prev else None
    oin_ref = next(it) if has_in else None
    lin_ref = next(it) if has_in else None
    oout_ref = next(it)
    lout_ref = None if final else next(it)

    mb = pl.program_id(2)
    r = pl.program_id(3)
    rows = pl.ds(r, BLOCK, stride=dil) if dil > 1 else slice(None)
    nkeys = 2 * BLOCK if use_prev else BLOCK
    qi = lax.broadcasted_iota(jnp.int32, (BLOCK, nkeys), 0)
    kj = lax.broadcasted_iota(jnp.int32, (BLOCK, nkeys), 1)
    if use_prev:
        dist = qi + BLOCK - kj
        valid = (dist >= 0) & (dist <= BLOCK) & ((mb > 0) | (kj >= BLOCK))
    else:
        valid = qi >= kj
    lane = lax.broadcasted_iota(jnp.int32, (BLOCK, LANES), 1)
    nh = q_ref.shape[1] // HD_A
    nchunk = nh // A_LSE_CHUNK
    lse_tiles = [jnp.zeros((BLOCK, LANES), F32) for _ in range(nchunk)]
    lse_in_tiles = [lin_ref[c, rows, :] for c in range(nchunk)] if has_in else None
    for h in range(nh):
        hc, hl = divmod(h, A_LSE_CHUNK)
        sl = slice(h * HD_A, (h + 1) * HD_A)
        q = q_ref[:, sl]
        if use_prev:
            k = jnp.concatenate([kp_ref[:, sl], kc_ref[:, sl]], axis=0)
            v = jnp.concatenate([vp_ref[:, sl], vc_ref[:, sl]], axis=0)
        else:
            k = kc_ref[:, sl]
            v = vc_ref[:, sl]
        s = lax.dot_general(q, k, (((1,), (1,)), ((), ())), preferred_element_type=F32) * scale
        s = jnp.where(valid, s, NEG_INF)
        m = jnp.max(s, axis=1, keepdims=True)
        p = jnp.exp(s - m)
        l = jnp.sum(p, axis=1, keepdims=True)
        o = jnp.dot(p.astype(BF16), v, preferred_element_type=F32) / l
        lse = m + jnp.log(l)
        if has_in:
            lse_in = lse_in_tiles[hc][:, hl:hl + 1]
            top = jnp.maximum(lse_in, lse)
            lse_new = top + jnp.log(jnp.exp(lse_in - top) + jnp.exp(lse - top))
            o = jnp.exp(lse_in - lse_new) * oin_ref[h, rows, :] + jnp.exp(lse - lse_new) * o
            lse = lse_new
        if final:
            oout_ref[:, sl] = o.astype(oout_ref.dtype)
        else:
            oout_ref[h, rows, :] = o
            lse_tiles[hc] = jnp.where(lane == hl, lse, lse_tiles[hc])
    if not final:
        for c in range(nchunk):
            lout_ref[c, rows, :] = lse_tiles[c]


def attn_a_prompt(qkv_groups, *, nb, seq):
    m_rows = nb * seq
    gw = A_HG * HD_A
    o_state = lse_state = None
    order = sorted(range(A_GROUPS), key=lambda g: -A_CONFIGS[g][1])
    for step, gi in enumerate(order):
        win, dil = A_CONFIGS[gi]
        assert win // dil == BLOCK and seq % (dil * BLOCK) == 0
        nblk = seq // dil // BLOCK
        use_prev = nblk > 1
        has_in = step > 0
        final = step == A_GROUPS - 1
        assert not final or dil == 1
        qkv = qkv_groups[gi]
        nh = A_HG if BLOCK * dil <= 512 else A_LSE_CHUNK
        cw = nh * HD_A
        nhc = A_HG // nh
        nchunk = nh // A_LSE_CHUNK

        def seg_spec(seg, prev):
            if prev:
                return pl.BlockSpec((None, None, BLOCK, cw),
                                    lambda b, hc, mb, r, seg=seg: (b, r, jnp.maximum(mb - 1, 0), seg * nhc + hc))
            return pl.BlockSpec((None, None, BLOCK, cw), lambda b, hc, mb, r, seg=seg: (b, r, mb, seg * nhc + hc))

        o_spec = pl.BlockSpec((nh, BLOCK * dil, LANES), lambda b, hc, mb, r: (hc, b * nblk + mb, 0))
        l_spec = pl.BlockSpec((nchunk, BLOCK * dil, LANES), lambda b, hc, mb, r: (hc, b * nblk + mb, 0))
        args = [qkv, qkv]
        in_specs = [seg_spec(0, False), seg_spec(1, False)]
        if use_prev:
            args.append(qkv)
            in_specs.append(seg_spec(1, True))
        args.append(qkv)
        in_specs.append(seg_spec(2, False))
        if use_prev:
            args.append(qkv)
            in_specs.append(seg_spec(2, True))
        if has_in:
            args += [o_state, lse_state]
            in_specs += [o_spec, l_spec]
        if final:
            out_shape = [jax.ShapeDtypeStruct((m_rows, gw), BF16)]
            out_specs = [pl.BlockSpec((BLOCK, cw), lambda b, hc, mb, r: (b * nblk + mb, hc))]
        else:
            out_shape = [jax.ShapeDtypeStruct((A_HG, m_rows, LANES), F32),
                         jax.ShapeDtypeStruct((A_HG // A_LSE_CHUNK, m_rows, LANES), F32)]
            out_specs = [o_spec, l_spec]
        outs = pl.pallas_call(
            functools.partial(_attn_a_prompt_kernel, dil=dil, use_prev=use_prev, has_in=has_in, final=final,
                              scale=HD_A ** -0.5),
            out_shape=out_shape,
            grid=(nb, nhc, nblk, dil),
            in_specs=in_specs,
            out_specs=out_specs,
            compiler_params=_params("parallel", "parallel", "arbitrary", "arbitrary"),
            name=f"attn_a_prompt_g{gi}",
        )(*args)
        if final:
            return outs[0]
        o_state, lse_state = outs


def _attn_a_sample_kernel(q_ref, kn_ref, vn_ref, c1_ref, c2_ref, c3_ref, o_ref, *, t_len, scale):
    nh = A_HG
    cache_refs = (c1_ref, c2_ref, c3_ref)
    o_mix = [None] * t_len
    lse_mix = [None] * t_len
    for gi, (win, dil) in enumerate(A_CONFIGS):
        cref = cache_refs[gi]
        nkey = cref.shape[0]
        lb = nkey * dil
        t_sets = [list(range(t_len))] if dil == 1 else [[t] for t in range(t_len)]
        for ts in t_sets:
            nt = len(ts)
            res = (lb + ts[0]) % dil
            nrow = nt * nh
            qf = q_ref[ts[0]:ts[0] + nt, gi].reshape(nrow, HD_A).astype(BF16)
            kmat = cref[:, res, 0].reshape(nkey * nh, HD_A).astype(BF16)
            vmat = cref[:, res, 1].reshape(nkey * nh, HD_A).astype(BF16)
            s = lax.dot_general(qf, kmat, (((1,), (1,)), ((), ())), preferred_element_type=F32) * scale
            col = lax.broadcasted_iota(jnp.int32, s.shape, 1)
            rowi = lax.broadcasted_iota(jnp.int32, s.shape, 0)
            dist = lb + ts[0] + _div_pow2(rowi, nh) - (_div_pow2(col, nh) * dil + res)
            s = jnp.where((_mod_pow2(col, nh) == _mod_pow2(rowi, nh)) & (dist >= 0) & (dist <= win), s, NEG_INF)
            rowt = ts[0] + _div_pow2(lax.broadcasted_iota(jnp.int32, (nrow, 1), 0), nh)
            qf32 = qf.astype(F32)
            news = []
            for u in range(ts[-1] + 1):
                if not any((t - u) % dil == 0 and 0 <= t - u <= win for t in ts):
                    continue
                kn = jnp.tile(kn_ref[u, gi].astype(BF16).astype(F32), (nt, 1))
                ok = (rowt >= u) & (_mod_pow2(rowt - u, dil) == 0) & (rowt - u <= win)
                sn = jnp.where(ok, jnp.sum(qf32 * kn, axis=1, keepdims=True) * scale, NEG_INF)
                news.append((u, sn))
            m = jnp.max(s, axis=1, keepdims=True)
            for _, sn in news:
                m = jnp.maximum(m, sn)
            p = jnp.exp(s - m)
            l = jnp.sum(p, axis=1, keepdims=True)
            o = jnp.dot(p.astype(BF16), vmat, preferred_element_type=F32)
            for u, sn in news:
                pn = jnp.exp(sn - m)
                l = l + pn
                vn = jnp.tile(vn_ref[u, gi].astype(BF16).astype(F32), (nt, 1))
                o = o + pn.astype(BF16).astype(F32) * vn
            o = o / l
            lse = m + jnp.log(l)
            for a, t in enumerate(ts):
                o_t = o[a * nh:(a + 1) * nh]
                lse_t = lse[a * nh:(a + 1) * nh]
                if o_mix[t] is None:
                    o_mix[t], lse_mix[t] = o_t, lse_t
                else:
                    top = jnp.maximum(lse_mix[t], lse_t)
                    lse_new = top + jnp.log(jnp.exp(lse_mix[t] - top) + jnp.exp(lse_t - top))
                    o_mix[t] = jnp.exp(lse_mix[t] - lse_new) * o_mix[t] + jnp.exp(lse_t - lse_new) * o_t
                    lse_mix[t] = lse_new
    for t in range(t_len):
        o_ref[t] = o_mix[t]


def attn_a_sample(q, k_new, v_new, caches, *, t_len, nb):
    cache_args = []
    cache_specs = []
    for (win, dil), c in zip(A_CONFIGS, caches):
        lb = c.shape[1]
        assert lb % dil == 0 and (dil == 1 or t_len <= dil)
        nres = min(dil, t_len)
        cache_args.append(c.reshape(nb, lb // dil, dil, 2, A_HG, HD_A))
        cache_specs.append(pl.BlockSpec((None, lb // dil, nres, 2, A_HG, HD_A), lambda b: (b, 0, 0, 0, 0, 0)))
    row_spec = pl.BlockSpec((None, t_len, A_GROUPS, A_HG, HD_A), lambda b: (b, 0, 0, 0, 0))
    return pl.pallas_call(
        functools.partial(_attn_a_sample_kernel, t_len=t_len, scale=HD_A ** -0.5),
        out_shape=jax.ShapeDtypeStruct((nb, t_len, A_HG, HD_A), F32),
        grid=(nb,),
        in_specs=[row_spec, row_spec, row_spec] + cache_specs,
        out_specs=pl.BlockSpec((None, t_len, A_HG, HD_A), lambda b: (b, 0, 0, 0)),
        compiler_params=_params("parallel"),
        name="attn_a_sample",
    )(q, k_new, v_new, *cache_args)


def _attn_b_kernel(q_ref, kvc_ref, kvp_ref, sink_ref, o_ref, *, scale, tq, stack, first_block_has_prev):
    mb = pl.program_id(1)
    nrow = stack * tq
    kw = KV_B * HD_B
    kv = jnp.concatenate([kvp_ref[...], kvc_ref[...]], axis=0)
    nk = kv.shape[0]
    rowi = lax.broadcasted_iota(jnp.int32, (nrow, nk), 0)
    kj = lax.broadcasted_iota(jnp.int32, (nrow, nk), 1)
    dist = _mod_pow2(rowi, tq) + BLOCK - kj
    valid = (dist >= 0) & (dist <= WINDOW_B)
    if not first_block_has_prev:
        valid = valid & ((mb > 0) | (kj >= BLOCK))
    lane = lax.broadcasted_iota(jnp.int32, (nk, LANES), 1)
    lane_q = lax.broadcasted_iota(jnp.int32, (nrow, LANES), 1)
    pair_of_row = _div_pow2(lax.broadcasted_iota(jnp.int32, (nrow, 1), 0), tq)
    tiles_per_kvh = q_ref.shape[1] // LANES // KV_B
    for kvh in range(KV_B):
        pc = (kvh * HD_B // LANES) * LANES
        kpair = kv[:, pc:pc + LANES]
        vpair = kv[:, kw + pc:kw + pc + LANES]
        if (kvh * HD_B) % LANES == 0:
            k_lo = jnp.where(lane < HD_B, kpair, 0.0)
            v_lo = jnp.where(lane < HD_B, vpair, 0.0)
            k_hi = pltpu.roll(k_lo, HD_B, 1)
            v_hi = pltpu.roll(v_lo, HD_B, 1)
        else:
            k_hi = jnp.where(lane >= HD_B, kpair, 0.0)
            v_hi = jnp.where(lane >= HD_B, vpair, 0.0)
            k_lo = pltpu.roll(k_hi, HD_B, 1)
            v_lo = pltpu.roll(v_hi, HD_B, 1)
        k2 = jnp.concatenate([k_lo, k_hi], axis=0).astype(BF16)
        v2 = jnp.concatenate([v_lo, v_hi], axis=0).astype(BF16)
        for tile in range(tiles_per_kvh):
            col = (kvh * tiles_per_kvh + tile) * LANES
            q = q_ref[:, col:col + LANES]
            s = lax.dot_general(q, k2, (((1,), (1,)), ((), ())), preferred_element_type=F32) * scale
            ps = []
            inv = []
            for half in range(2):
                head0 = ((kvh * tiles_per_kvh + tile) * stack) * 2 + half
                sink = jnp.full((nrow, 1), sink_ref[head0], F32)
                for pr in range(1, stack):
                    sink = jnp.where(pair_of_row == pr, sink_ref[head0 + 2 * pr], sink)
                sh = jnp.where(valid, s[:, half * nk:(half + 1) * nk], NEG_INF)
                m = jnp.maximum(jnp.max(sh, axis=1, keepdims=True), sink)
                p = jnp.exp(sh - m)
                l = jnp.sum(p, axis=1, keepdims=True) + jnp.exp(sink - m)
                ps.append(p)
                inv.append(1.0 / l)
            p2 = jnp.concatenate(ps, axis=1).astype(BF16)
            o = jnp.dot(p2, v2, preferred_element_type=F32)
            o_ref[:, col:col + LANES] = (o * jnp.where(lane_q < HD_B, inv[0], inv[1])).astype(o_ref.dtype)


def attn_b(q, kv_cur, kv_prev, sinks, *, nb, nblk, tq, stack, first_block_has_prev):
    dq = q.shape[2]
    dkv = kv_cur.shape[2]
    if kv_prev is None:
        prev_arg = kv_cur
        prev_spec = pl.BlockSpec((None, BLOCK, dkv), lambda b, mb: (b, jnp.maximum(mb - 1, 0), 0))
    else:
        assert nblk == 1
        prev_arg = kv_prev
        prev_spec = pl.BlockSpec((None, BLOCK, dkv), lambda b, mb: (b, 0, 0))
    return pl.pallas_call(
        functools.partial(_attn_b_kernel, scale=HD_B ** -0.5, tq=tq, stack=stack,
                          first_block_has_prev=first_block_has_prev),
        out_shape=jax.ShapeDtypeStruct(q.shape, BF16),
        grid=(nb, nblk),
        in_specs=[pl.BlockSpec((None, stack * tq, dq), lambda b, mb: (b, mb, 0)),
                  pl.BlockSpec((None, BLOCK, dkv), lambda b, mb: (b, mb, 0)),
                  prev_spec,
                  pl.BlockSpec(memory_space=pltpu.SMEM)],
        out_specs=pl.BlockSpec((None, stack * tq, dq), lambda b, mb: (b, mb, 0)),
        compiler_params=_params("parallel", "arbitrary"),
        name="attn_b",
    )(q, kv_cur, prev_arg, sinks)


def _cast_pad_kernel(w_ref, o_ref, *, rows, cols):
    x = w_ref[...]
    tr, tc = x.shape
    if rows % tr or cols % tc:
        r = lax.broadcasted_iota(jnp.int32, x.shape, 0) + pl.program_id(1) * tr
        c = lax.broadcasted_iota(jnp.int32, x.shape, 1) + pl.program_id(2) * tc
        x = jnp.where((r < rows) & (c < cols), x, 0.0)
    o_ref[...] = x.astype(o_ref.dtype)


def cast_pad(w, rows_pad, cols_pad, *, tr=512, tc=1024):
    nl, rows, cols = w.shape
    return pl.pallas_call(
        functools.partial(_cast_pad_kernel, rows=rows, cols=cols),
        out_shape=jax.ShapeDtypeStruct((nl, rows_pad, cols_pad), BF16),
        grid=(nl, rows_pad // tr, cols_pad // tc),
        in_specs=[pl.BlockSpec((None, tr, tc), lambda l, i, j: (l, i, j))],
        out_specs=pl.BlockSpec((None, tr, tc), lambda l, i, j: (l, i, j)),
        compiler_params=_params("parallel", "parallel", "parallel"),
        name="cast_pad",
    )(w)


def _pad_last(w, n):
    return jnp.pad(w, [(0, 0)] * (w.ndim - 1) + [(0, n - w.shape[-1])])


def kernel(x_prompt, x_sample, cache_a1, cache_a2, cache_a3, cache_b, state_conv, norm_attn, norm_ffn,
           w_qkv_a, w_o_a, norm_kv, w_kv_b, w_q_b, sinks_b, w_o_b, w_gate, w_up, conv_w, conv_b, w_down,
           norm_final):
    nbp, seq, d = x_prompt.shape
    nbs, t_len, _ = x_sample.shape
    depth = norm_attn.shape[0]
    n_a = w_qkv_a.shape[0]
    d_ff = w_gate.shape[2]
    f_pad = -(-d_ff // FF_ALIGN) * FF_ALIGN
    gw = A_HG * HD_A
    nq = A_GROUPS * gw
    kw = KV_B * HD_B
    mp = nbp * seq
    ms = nbs * t_len
    tm_p = 1024
    tiles_per_seq = seq // tm_p
    a_caches = (cache_a1, cache_a2, cache_a3)
    shift_a = HD_A // ROT_FRAC // 2
    shift_b = HD_B // ROT_FRAC // 2

    pos_p = jnp.arange(seq, dtype=jnp.int32)
    pos_s = jnp.repeat(PAST_LEN + jnp.arange(t_len, dtype=jnp.int32), nbs)
    tabs_a_p, tabs_a_s = rope_tables(pos_p, HD_A), rope_tables(pos_s, HD_A)
    tabs_b_p, tabs_b_s = rope_tables(pos_p, HD_B), rope_tables(pos_s, HD_B)

    hp = x_prompt.reshape(mp, d)
    hs = x_sample.transpose(1, 0, 2).reshape(ms, d)

    dq = w_q_b.shape[2]
    wqkv_a = cast_pad(w_qkv_a, d, 3 * nq)
    wo_a = cast_pad(w_o_a, gw, d)
    wkv_b = cast_pad(w_kv_b[None], d, 2 * kw)
    wq_b = cast_pad(w_q_b, d, dq)
    wo_b = cast_pad(w_o_b, dq, d)
    wg = cast_pad(w_gate, d, f_pad)
    wu = cast_pad(w_up, d, f_pad)
    wd = cast_pad(w_down, f_pad, d)
    cw = _pad_last(conv_w, f_pad)
    cb = _pad_last(conv_b[:, None], f_pad)

    new_a_p = [[] for _ in range(A_GROUPS)]
    new_a_s = [[] for _ in range(A_GROUPS)]
    conv_p, conv_s = [], []
    kv_p = kv_s = new_b_s = None

    for layer in range(depth):
        if layer < n_a:
            wo, wo_l = wo_a, layer
            xp, = rmsnorm(hp, norm_attn[layer][None], BF16, 512)
            xs, = rmsnorm(hs, norm_attn[layer][None], BF16, ms)
            qkv_groups = []
            for gi, (win, dil) in enumerate(A_CONFIGS):
                qkv_g, kvf_g = proj(xp, wqkv_a, layer, tabs_a_p, segs=[gi * gw, nq + gi * gw, 2 * nq + gi * gw],
                                    seg_cols=gw, tm=tm_p, tn=512, tiles_per_seq=tiles_per_seq,
                                    n_rope_segs=2, f32_from_seg=1, shift=shift_a, dil=dil)
                qkv_groups.append(qkv_g.reshape(nbp, dil, seq // dil, 3 * gw))
                new_a_p[gi].append(kvf_g.reshape(nbp, seq, 2, A_HG, HD_A)[:, -min(win, seq):])
            ap = attn_a_prompt(qkv_groups, nb=nbp, seq=seq)
            qkv_s = proj(xs, wqkv_a, layer, tabs_a_s, segs=[0, nq, 2 * nq], seg_cols=nq, tm=ms, tn=512, tiles_per_seq=1,
                         n_rope_segs=2, f32_from_seg=0, shift=shift_a, has_bf=False)
            qkv_s = qkv_s.reshape(t_len, nbs, 3, A_GROUPS, A_HG, HD_A).transpose(2, 1, 0, 3, 4, 5)
            as_ = attn_a_sample(qkv_s[0], qkv_s[1], qkv_s[2], [c[layer] for c in a_caches], t_len=t_len, nb=nbs)
            as_ = as_.transpose(1, 0, 2, 3).reshape(ms, gw).astype(BF16)
            for gi in range(A_GROUPS):
                kv_new = jnp.stack([qkv_s[1][:, :, gi], qkv_s[2][:, :, gi]], axis=2)
                new_a_s[gi].append(jnp.concatenate([a_caches[gi][layer][:, t_len:], kv_new], axis=1))
        else:
            lb = layer - n_a
            wo, wo_l = wo_b, lb
            if layer == n_a:
                gains = jnp.stack([norm_attn[layer], norm_kv])
                xp, xkv_p = rmsnorm(hp, gains, BF16, 512)
                xs, xkv_s = rmsnorm(hs, gains, BF16, ms)
                kv_p = proj(xkv_p, wkv_b, 0, tabs_b_p, segs=[0, kw], seg_cols=kw, tm=tm_p, tn=kw,
                            tiles_per_seq=tiles_per_seq, n_rope_segs=1, f32_from_seg=0, shift=shift_b, has_bf=False)
                kv_s = proj(xkv_s, wkv_b, 0, tabs_b_s, segs=[0, kw], seg_cols=kw, tm=ms, tn=kw,
                            tiles_per_seq=1, n_rope_segs=1, f32_from_seg=0, shift=shift_b, has_bf=False)
                kv_s5 = kv_s.reshape(t_len, nbs, 2, KV_B, HD_B).transpose(1, 0, 2, 3, 4)
                new_b_s = jnp.concatenate([cache_b[:, t_len:], kv_s5], axis=1)
            else:
                xp, = rmsnorm(hp, norm_attn[layer][None], BF16, 512)
                xs, = rmsnorm(hs, norm_attn[layer][None], BF16, ms)
            q_p = proj(xp, wq_b, lb, tabs_b_p, segs=[0], seg_cols=dq, tm=tm_p, tn=512, tiles_per_seq=tiles_per_seq,
                       n_rope_segs=1, f32_from_seg=None, shift=shift_b)
            q_s = proj(xs, wq_b, lb, tabs_b_s, segs=[0], seg_cols=dq, tm=ms, tn=512, tiles_per_seq=1,
                       n_rope_segs=1, f32_from_seg=None, shift=shift_b)
            ap = attn_b(q_p.reshape(nbp, seq, dq), kv_p.reshape(nbp, seq, 2 * kw), None, sinks_b[lb],
                        nb=nbp, nblk=seq // BLOCK, tq=BLOCK, stack=1, first_block_has_prev=False).reshape(mp, dq)
            npair = dq // LANES // KV_B
            q_sb = q_s.reshape(t_len, nbs, KV_B, npair, LANES).transpose(1, 3, 0, 2, 4)
            q_sb = q_sb.reshape(nbs, npair * t_len, KV_B * LANES)
            kv_sb = jnp.pad(kv_s.reshape(t_len, nbs, 2 * kw).transpose(1, 0, 2), ((0, 0), (0, BLOCK - t_len), (0, 0)))
            o_sb = attn_b(q_sb, kv_sb, cache_b.reshape(nbs, WINDOW_B, 2 * kw), sinks_b[lb],
                          nb=nbs, nblk=1, tq=t_len, stack=npair, first_block_has_prev=True)
            as_ = o_sb.reshape(nbs, npair, t_len, KV_B, LANES).transpose(2, 0, 3, 1, 4).reshape(ms, dq)
        hp = mm_res(ap, wo, wo_l, hp, tm=1024, tn=1024, tk=wo.shape[1])
        hs = mm_res(as_, wo, wo_l, hs, tm=ms, tn=1024, tk=wo.shape[1])

        xp, = rmsnorm(hp, norm_ffn[layer][None], BF16, 512)
        xs, = rmsnorm(hs, norm_ffn[layer][None], BF16, ms)
        act_p, tails = ffn_up_prompt(xp, wg, wu, cw, cb, layer, tm=tm_p, tf=512, tiles_per_seq=tiles_per_seq)
        prev_s = _pad_last(state_conv[layer].transpose(1, 0, 2).reshape((CONV_W - 1) * nbs, d_ff), f_pad)
        act_s, tail_s = ffn_up_sample(xs, wg, wu, cw, cb, layer, prev_s, tf=1024, nb=nbs)
        conv_p.append(tails.reshape(nbp, tiles_per_seq, 8, f_pad)[:, -1, 8 - (CONV_W - 1):, :d_ff])
        conv_s.append(tail_s.reshape(CONV_W - 1, nbs, f_pad).transpose(1, 0, 2)[:, :, :d_ff])
        tk_down = f_pad // 4
        hp = mm_res(act_p, wd, layer, hp, tm=1024, tn=1024, tk=tk_down)
        hs = mm_res(act_s, wd, layer, hs, tm=ms, tn=1024, tk=tk_down)

    y_p, = rmsnorm(hp, norm_final[None], F32, 512)
    y_s, = rmsnorm(hs, norm_final[None], F32, ms)
    y_prompt = y_p.reshape(nbp, seq, d)
    y_sample = y_s.reshape(t_len, nbs, d).transpose(1, 0, 2)

    new_b_p = kv_p.reshape(nbp, seq, 2, KV_B, HD_B)[:, -min(WINDOW_B, seq):]
    outs = [y_prompt, y_sample]
    for gi in range(A_GROUPS):
        outs += [jnp.stack(new_a_p[gi], axis=0), jnp.stack(new_a_s[gi], axis=0)]
    outs += [new_b_p, new_b_s, jnp.stack(conv_p, axis=0), jnp.stack(conv_s, axis=0)]
    return tuple(outs)
```

```python
import functools

import jax
import jax.numpy as jnp
from jax import lax
from jax.experimental import pallas as pl
from jax.experimental.pallas import tpu as pltpu

F32 = jnp.float32
BF16 = jnp.bfloat16

HD_A = 128
A_HG = 16
A_CONFIGS = ((128, 1), (512, 4), (2048, 16))
A_GROUPS = len(A_CONFIGS)
A_LSE_CHUNK = 8
HD_B = 64
KV_B = 8
WINDOW_B = 128
PAST_LEN = 16384
CONV_W = 3
ROPE_THETA = 500000.0
ROT_FRAC = 4
NORM_EPS = 1e-5
BLOCK = 128
NEG_INF = -1e30
LANES = 128
FF_ALIGN = 1024
FFN_ROWS = 64
CAST_TILE = (512, 1024)
CAST_CHUNK = (128, 128)
VMEM_LIMIT = 60 * 1024 * 1024


def _params(*sem):
    return pltpu.CompilerParams(dimension_semantics=sem, vmem_limit_bytes=VMEM_LIMIT)


def _div_pow2(x, n):
    assert n & (n - 1) == 0
    return x >> (n.bit_length() - 1)


def _mod_pow2(x, n):
    assert n & (n - 1) == 0
    return x & (n - 1)


def _cast_tile(x, row0, col0, rows, cols, padded):
    if padded:
        r = lax.broadcasted_iota(jnp.int32, x.shape, 0) + row0
        c = lax.broadcasted_iota(jnp.int32, x.shape, 1) + col0
        x = jnp.where((r < rows) & (c < cols), x, 0.0)
    return x.astype(BF16)


class _Ride:
    def __init__(self, w, layer, rows_pad, cols_pad, lin, nsteps):
        self.w, self.layer, self.lin = w, layer, lin
        _, self.rows, self.cols = w.shape
        self.rows_pad, self.cols_pad = rows_pad, cols_pad
        self.ncol = cols_pad // CAST_TILE[1]
        self.ntile = (rows_pad // CAST_TILE[0]) * self.ncol
        assert rows_pad % CAST_TILE[0] == 0 and cols_pad % CAST_TILE[1] == 0 and nsteps >= self.ntile

    def tile(self, step):
        t = jnp.minimum(step, self.ntile - 1)
        return t // self.ncol, t % self.ncol

    @property
    def in_spec(self):
        return pl.BlockSpec((None,) + CAST_TILE, lambda *g: (self.layer,) + self.tile(self.lin(*g)))

    @property
    def out_spec(self):
        return pl.BlockSpec(CAST_TILE, lambda *g: self.tile(self.lin(*g)))

    @property
    def out_shape(self):
        return jax.ShapeDtypeStruct((self.rows_pad, self.cols_pad), BF16)

    def cast(self, w_ref, o_ref, step):
        ti, tj = self.tile(step)
        for r0 in range(0, CAST_TILE[0], CAST_CHUNK[0]):
            for c0 in range(0, CAST_TILE[1], CAST_CHUNK[1]):
                sl = (slice(r0, r0 + CAST_CHUNK[0]), slice(c0, c0 + CAST_CHUNK[1]))
                o_ref[sl] = _cast_tile(w_ref[sl], ti * CAST_TILE[0] + r0, tj * CAST_TILE[1] + c0, self.rows,
                                       self.cols, (self.rows_pad, self.cols_pad) != (self.rows, self.cols))


def _rmsnorm_kernel(h_ref, g_ref, *o_refs):
    x = h_ref[...]
    y = x * lax.rsqrt(jnp.mean(x * x, axis=-1, keepdims=True) + NORM_EPS)
    for n, o_ref in enumerate(o_refs):
        o_ref[...] = (y * g_ref[n:n + 1, :]).astype(o_ref.dtype)


def rmsnorm(h, gains, out_dtype, tm):
    m, d = h.shape
    n = gains.shape[0]
    return pl.pallas_call(
        _rmsnorm_kernel,
        out_shape=[jax.ShapeDtypeStruct((m, d), out_dtype)] * n,
        grid=(m // tm,),
        in_specs=[pl.BlockSpec((tm, d), lambda i: (i, 0)),
                  pl.BlockSpec((n, d), lambda i: (0, 0))],
        out_specs=[pl.BlockSpec((tm, d), lambda i: (i, 0))] * n,
        compiler_params=_params("parallel"),
        name="rmsnorm",
    )(h, gains)


def rope_tables(pos, hd):
    rot = hd // ROT_FRAC
    half = rot // 2
    inv = ROPE_THETA ** (-jnp.arange(half, dtype=F32) * 2.0 / rot)
    ang = pos.astype(F32)[:, None] * inv[None, :]
    cos, sin = jnp.cos(ang), jnp.sin(ang)
    s = pos.shape[0]
    ones = jnp.ones((s, hd - rot), F32)
    zrest = jnp.zeros((s, hd - rot), F32)
    zhalf = jnp.zeros((s, half), F32)
    reps = LANES // hd
    cos_t = jnp.tile(jnp.concatenate([cos, cos, ones], axis=1), (1, reps))
    sa_t = jnp.tile(jnp.concatenate([-sin, zhalf, zrest], axis=1), (1, reps))
    sb_t = jnp.tile(jnp.concatenate([zhalf, sin, zrest], axis=1), (1, reps))
    return (jnp.stack([cos_t, jnp.ones_like(cos_t)]), jnp.stack([sa_t, jnp.zeros_like(sa_t)]),
            jnp.stack([sb_t, jnp.zeros_like(sb_t)]))


def _proj_kernel(x_ref, w_ref, cos_ref, sa_ref, sb_ref, *refs, shift, dil, has_bf, has_f32, ride):
    refs = list(refs)
    ride_w_ref = refs.pop(0) if ride else None
    obf_ref = refs.pop(0) if has_bf else None
    of32_ref = refs.pop(0) if has_f32 else None
    ride_o_ref = refs.pop(0) if ride else None
    acc_refs = (refs.pop(0), refs.pop(0))
    slab_ref = refs.pop(0) if dil > 1 else None
    s = pl.program_id(0)
    tm, tn = acc_refs[0].shape

    @pl.when(s == 0)
    def _():
        acc_refs[1][...] = jnp.zeros((tm, tn), F32)

    def finish(acc_ref):
        for c in range(tn // LANES):
            sl = slice(c * LANES, (c + 1) * LANES)
            xh = acc_ref[:, sl]
            xh = (xh * cos_ref[...] + pltpu.roll(xh, LANES - shift, 1) * sa_ref[...]
                  + pltpu.roll(xh, shift, 1) * sb_ref[...])
            if has_f32:
                of32_ref[:, sl] = xh
            if not has_bf:
                continue
            if dil == 1:
                obf_ref[:, sl] = xh.astype(obf_ref.dtype)
            else:
                slab_ref[...] = xh
                for r in range(dil):
                    obf_ref[r, :, sl] = slab_ref[pl.ds(r, tm // dil, stride=dil), :].astype(obf_ref.dtype)

    for parity in range(2):
        @pl.when(s % 2 == parity)
        def _(parity=parity):
            if ride:
                ride.cast(ride_w_ref, ride_o_ref, s)
            acc_refs[parity][...] = jnp.dot(x_ref[...], w_ref[...], preferred_element_type=F32)
            finish(acc_refs[1 - parity])


def proj(xn, w, layer, tables, *, segs, seg_cols, tm, tn, tiles_per_seq, n_rope_segs, f32_from_seg, shift,
         dil=1, has_bf=True, ride_w=None):
    m, k = xn.shape
    per_seg = seg_cols // tn
    nj = len(segs) * per_seg
    ncols = len(segs) * seg_cols
    starts = [s // tn for s in segs]
    step = starts[1] - starts[0] if len(segs) > 1 else 0
    assert all(starts[a] == starts[0] + a * step for a in range(len(segs)))
    has_f32 = f32_from_seg is not None
    f32_from = f32_from_seg * per_seg if has_f32 else nj
    n_rope = n_rope_segs * per_seg
    ntiles = (m // tm) * nj

    def cur(s):
        t = jnp.minimum(s, ntiles - 1)
        return t // nj, t % nj

    def prev(s):
        t = jnp.maximum(s - 1, 0)
        return t // nj, t % nj

    def w_map(s):
        _, j = cur(s)
        return layer, 0, starts[0] + (j // per_seg) * step + j % per_seg

    def tab_map(s):
        i, j = prev(s)
        return jnp.where(j < n_rope, 0, 1), i % tiles_per_seq, 0

    def bf_map(s):
        i, j = prev(s)
        return (i, j) if dil == 1 else (i // tiles_per_seq, 0, i % tiles_per_seq, j)

    def f32_map(s):
        i, j = prev(s)
        return i, jnp.maximum(j - f32_from, 0)

    out_shape, out_specs = [], []
    if has_bf:
        if dil == 1:
            out_shape.append(jax.ShapeDtypeStruct((m, ncols), BF16))
            out_specs.append(pl.BlockSpec((tm, tn), bf_map))
        else:
            nseq = m // (tm * tiles_per_seq)
            out_shape.append(jax.ShapeDtypeStruct((nseq, dil, tm * tiles_per_seq // dil, ncols), BF16))
            out_specs.append(pl.BlockSpec((None, dil, tm // dil, tn), bf_map))
    if has_f32:
        out_shape.append(jax.ShapeDtypeStruct((m, ncols - f32_from * tn), F32))
        out_specs.append(pl.BlockSpec((tm, tn), f32_map))
    tab_spec = pl.BlockSpec((None, tm, LANES), tab_map)
    in_specs = [pl.BlockSpec((tm, k), lambda s: (cur(s)[0], 0)),
                pl.BlockSpec((None, k, tn), w_map),
                tab_spec, tab_spec, tab_spec]
    args = [xn, w, *tables]
    ride = None
    if ride_w is not None:
        ride = _Ride(*ride_w, lin=lambda s: s, nsteps=ntiles + 1)
        in_specs.append(ride.in_spec)
        args.append(ride.w)
        out_shape.append(ride.out_shape)
        out_specs.append(ride.out_spec)
    outs = pl.pallas_call(
        functools.partial(_proj_kernel, shift=shift, dil=dil, has_bf=has_bf, has_f32=has_f32, ride=ride),
        out_shape=out_shape,
        grid=(ntiles + 1,),
        in_specs=in_specs,
        out_specs=out_specs,
        scratch_shapes=[pltpu.VMEM((tm, tn), F32), pltpu.VMEM((tm, tn), F32)]
        + ([pltpu.VMEM((tm, LANES), F32)] if dil > 1 else []),
        compiler_params=_params("arbitrary"),
        name="proj",
    )(*args)
    return outs if len(outs) > 1 else outs[0]


def _mm_res_kernel(x_ref, w_ref, res_ref, *refs, ride):
    refs = list(refs)
    ride_w_ref = refs.pop(0) if ride else None
    o_ref = refs.pop(0)

    @pl.when(pl.program_id(2) == 0)
    def _():
        o_ref[...] = res_ref[...]

    o_ref[...] += jnp.dot(x_ref[...], w_ref[...], preferred_element_type=F32)
    if ride:
        ride.cast(ride_w_ref, refs.pop(0), ride.lin(pl.program_id(0), pl.program_id(1), pl.program_id(2)))


def mm_res(x, w, layer, res, *, tm, tn, tk, ride_w=None):
    m, kdim = x.shape
    n = w.shape[2]
    grid = (m // tm, n // tn, kdim // tk)
    in_specs = [pl.BlockSpec((tm, tk), lambda i, j, k: (i, k)),
                pl.BlockSpec((None, tk, tn), lambda i, j, k: (layer, k, j)),
                pl.BlockSpec((tm, tn), lambda i, j, k: (i, j))]
    args = [x, w, res]
    out_shape = [jax.ShapeDtypeStruct((m, n), F32)]
    out_specs = [pl.BlockSpec((tm, tn), lambda i, j, k: (i, j))]
    ride = None
    if ride_w is not None:
        ride = _Ride(*ride_w, lin=lambda i, j, k: (i * grid[1] + j) * grid[2] + k, nsteps=grid[0] * grid[1] * grid[2])
        in_specs.append(ride.in_spec)
        args.append(ride.w)
        out_shape.append(ride.out_shape)
        out_specs.append(ride.out_spec)
    outs = pl.pallas_call(
        functools.partial(_mm_res_kernel, ride=ride),
        out_shape=out_shape,
        grid=grid,
        in_specs=in_specs,
        out_specs=out_specs,
        compiler_params=_params("parallel" if ride is None else "arbitrary",
                                "parallel" if ride is None else "arbitrary", "arbitrary"),
        name="mm_res",
    )(*args)
    return outs if ride else outs[0]


def _silu(c):
    return c * jax.nn.sigmoid(c)


def _ffn_up_prompt_kernel(x_ref, wg_ref, wu_ref, cw_ref, cb_ref, *refs, nj, tiles_per_seq, ride):
    refs = list(refs)
    ride_w_ref = refs.pop(0) if ride else None
    act_ref, tail_ref = refs.pop(0), refs.pop(0)
    ride_o_ref = refs.pop(0) if ride else None
    g0_ref, u0_ref, g1_ref, u1_ref, carry_ref = refs
    s = pl.program_id(0)
    t_prev = jnp.maximum(s - 1, 0)
    ip = t_prev // nj
    jp = t_prev % nj
    tm, tf = g0_ref.shape

    @pl.when(s == 0)
    def _():
        g1_ref[...] = jnp.zeros((tm, tf), F32)
        u1_ref[...] = jnp.zeros((tm, tf), F32)

    @pl.when((ip % tiles_per_seq) == 0)
    def _():
        carry_ref[jp] = jnp.zeros(carry_ref.shape[1:], F32)

    def finish(g_ref, u_ref):
        for c0 in range(0, tf, LANES):
            sl = slice(c0, c0 + LANES)
            cw0, cw1, cw2, cbv = cw_ref[0:1, sl], cw_ref[1:2, sl], cw_ref[2:3, sl], cb_ref[:, sl]
            prev = carry_ref[jp, :, sl]
            for a in range(0, tm, FFN_ROWS):
                if a == 0:
                    ext = jnp.concatenate([prev, g_ref[0:FFN_ROWS, sl]], axis=0)
                else:
                    ext = g_ref[a - 8:a + FFN_ROWS, sl]
                c = cbv + cw0 * pltpu.roll(ext, 2, 0)[8:] + cw1 * pltpu.roll(ext, 1, 0)[8:] + cw2 * ext[8:]
                act_ref[a:a + FFN_ROWS, sl] = (_silu(c) * u_ref[a:a + FFN_ROWS, sl]).astype(act_ref.dtype)
            last = g_ref[tm - 8:tm, sl]
            carry_ref[jp, :, sl] = last
            tail_ref[:, sl] = last

    bufs = ((g0_ref, u0_ref), (g1_ref, u1_ref))
    for parity in range(2):
        @pl.when(s % 2 == parity)
        def _(parity=parity):
            x = x_ref[...]
            bufs[parity][0][...] = jnp.dot(x, wg_ref[...], preferred_element_type=F32)
            bufs[parity][1][...] = jnp.dot(x, wu_ref[...], preferred_element_type=F32)
            finish(*bufs[1 - parity])
            if ride:
                ride.cast(ride_w_ref, ride_o_ref, s)


def ffn_up_prompt(xn, wg, wu, wlayer, cw, cb, layer, *, tm, tf, tiles_per_seq, ride_w=None):
    m, d = xn.shape
    f = wg.shape[2]
    nj = f // tf
    ntiles = (m // tm) * nj

    def cur(s):
        t = jnp.minimum(s, ntiles - 1)
        return t // nj, t % nj

    def prev(s):
        t = jnp.maximum(s - 1, 0)
        return t // nj, t % nj

    buf = pltpu.VMEM((tm, tf), F32)
    in_specs = [pl.BlockSpec((tm, d), lambda s: (cur(s)[0], 0)),
                pl.BlockSpec((None, d, tf), lambda s: (wlayer, 0, cur(s)[1])),
                pl.BlockSpec((None, d, tf), lambda s: (wlayer, 0, cur(s)[1])),
                pl.BlockSpec((None, CONV_W, tf), lambda s: (layer, 0, prev(s)[1])),
                pl.BlockSpec((None, 1, tf), lambda s: (layer, 0, prev(s)[1]))]
    args = [xn, wg, wu, cw, cb]
    out_shape = [jax.ShapeDtypeStruct((m, f), BF16), jax.ShapeDtypeStruct((m // tm, 8, f), F32)]
    out_specs = [pl.BlockSpec((tm, tf), lambda s: prev(s)),
                 pl.BlockSpec((None, 8, tf), lambda s: (prev(s)[0], 0, prev(s)[1]))]
    ride = None
    if ride_w is not None:
        ride = _Ride(*ride_w, lin=lambda s: s, nsteps=ntiles + 1)
        in_specs.append(ride.in_spec)
        args.append(ride.w)
        out_shape.append(ride.out_shape)
        out_specs.append(ride.out_spec)
    return pl.pallas_call(
        functools.partial(_ffn_up_prompt_kernel, nj=nj, tiles_per_seq=tiles_per_seq, ride=ride),
        out_shape=out_shape,
        grid=(ntiles + 1,),
        in_specs=in_specs,
        out_specs=out_specs,
        scratch_shapes=[buf, buf, buf, buf, pltpu.VMEM((nj, 8, tf), F32)],
        compiler_params=_params("arbitrary"),
        name="ffn_up_prompt",
    )(*args)


def _ffn_up_sample_kernel(x_ref, wg_ref, wu_ref, cw_ref, cb_ref, prev_ref, act_ref, tail_ref, *, nb):
    x = x_ref[...]
    g = jnp.dot(x, wg_ref[...], preferred_element_type=F32)
    u = jnp.dot(x, wu_ref[...], preferred_element_type=F32)
    m = g.shape[0]
    gext = jnp.concatenate([prev_ref[...], g], axis=0)
    c = (cb_ref[...] + cw_ref[0:1, :] * gext[0:m] + cw_ref[1:2, :] * gext[nb:nb + m]
         + cw_ref[2:3, :] * gext[2 * nb:2 * nb + m])
    act_ref[...] = (_silu(c) * u).astype(act_ref.dtype)
    tail_ref[...] = gext[m:, :]


def ffn_up_sample(xn, wg, wu, wlayer, cw, cb, layer, prev, *, tf, nb):
    m, d = xn.shape
    f = wg.shape[2]
    return pl.pallas_call(
        functools.partial(_ffn_up_sample_kernel, nb=nb),
        out_shape=[jax.ShapeDtypeStruct((m, f), BF16),
                   jax.ShapeDtypeStruct((2 * nb, f), F32)],
        grid=(f // tf,),
        in_specs=[pl.BlockSpec((m, d), lambda j: (0, 0)),
                  pl.BlockSpec((None, d, tf), lambda j: (wlayer, 0, j)),
                  pl.BlockSpec((None, d, tf), lambda j: (wlayer, 0, j)),
                  pl.BlockSpec((None, CONV_W, tf), lambda j: (layer, 0, j)),
                  pl.BlockSpec((None, 1, tf), lambda j: (layer, 0, j)),
                  pl.BlockSpec((2 * nb, tf), lambda j: (0, j))],
        out_specs=[pl.BlockSpec((m, tf), lambda j: (0, j)),
                   pl.BlockSpec((2 * nb, tf), lambda j: (0, j))],
        compiler_params=_params("parallel"),
        name="ffn_up_sample",
    )(xn, wg, wu, cw, cb, prev)


def _attn_a_prompt_kernel(*refs, dil, use_prev, has_in, final, scale):
    it = iter(refs)
    q_ref = next(it)
    kc_ref = next(it)
    kp_ref = next(it) if use_prev else None
    vc_ref = next(it)
    vp_ref = next(it) if use_prev else None
    oin_ref = next(it) if has_in else None
    lin_ref = next(it) if has_in else None
    oout_ref = next(it)
    lout_ref = None if final else next(it)

    mb = pl.program_id(2)
    r = pl.program_id(3)
    rows = pl.ds(r, BLOCK, stride=dil) if dil > 1 else slice(None)
    nkeys = 2 * BLOCK if use_prev else BLOCK
    qi = lax.broadcasted_iota(jnp.int32, (BLOCK, nkeys), 0)
    kj = lax.broadcasted_iota(jnp.int32, (BLOCK, nkeys), 1)
    if use_prev:
        dist = qi + BLOCK - kj
        valid = (dist >= 0) & (dist <= BLOCK) & ((mb > 0) | (kj >= BLOCK))
    else:
        valid = qi >= kj
    lane = lax.broadcasted_iota(jnp.int32, (BLOCK, LANES), 1)
    nh = q_ref.shape[1] // HD_A
    nchunk = nh // A_LSE_CHUNK
    lse_tiles = [jnp.zeros((BLOCK, LANES), F32) for _ in range(nchunk)]
    lse_in_tiles = [lin_ref[c, rows, :] for c in range(nchunk)] if has_in else None
    for h in range(nh):
        hc, hl = divmod(h, A_LSE_CHUNK)
        sl = slice(h * HD_A, (h + 1) * HD_A)
        q = q_ref[:, sl]
        if use_prev:
            k = jnp.concatenate([kp_ref[:, sl], kc_ref[:, sl]], axis=0)
            v = jnp.concatenate([vp_ref[:, sl], vc_ref[:, sl]], axis=0)
        else:
            k = kc_ref[:, sl]
            v = vc_ref[:, sl]
        s = lax.dot_general(q, k, (((1,), (1,)), ((), ())), preferred_element_type=F32) * scale
        s = jnp.where(valid, s, NEG_INF)
        m = jnp.max(s, axis=1, keepdims=True)
        p = jnp.exp(s - m)
        l = jnp.sum(p, axis=1, keepdims=True)
        o = jnp.dot(p.astype(BF16), v, preferred_element_type=F32) / l
        lse = m + jnp.log(l)
        if has_in:
            lse_in = lse_in_tiles[hc][:, hl:hl + 1]
            top = jnp.maximum(lse_in, lse)
            lse_new = top + jnp.log(jnp.exp(lse_in - top) + jnp.exp(lse - top))
            o = jnp.exp(lse_in - lse_new) * oin_ref[h, rows, :] + jnp.exp(lse - lse_new) * o
            lse = lse_new
        if final:
            oout_ref[:, sl] = o.astype(oout_ref.dtype)
        else:
            oout_ref[h, rows, :] = o
            lse_tiles[hc] = jnp.where(lane == hl, lse, lse_tiles[hc])
    if not final:
        for c in range(nchunk):
            lout_ref[c, rows, :] = lse_tiles[c]


def attn_a_prompt(qkv_groups, *, nb, seq):
    m_rows = nb * seq
    gw = A_HG * HD_A
    o_state = lse_state = None
    order = sorted(range(A_GROUPS), key=lambda g: -A_CONFIGS[g][1])
    for step, gi in enumerate(order):
        win, dil = A_CONFIGS[gi]
        assert win // dil == BLOCK and seq % (dil * BLOCK) == 0
        nblk = seq // dil // BLOCK
        use_prev = nblk > 1
        has_in = step > 0
        final = step == A_GROUPS - 1
        assert not final or dil == 1
        qkv = qkv_groups[gi]
        nh = A_HG if BLOCK * dil <= 512 else A_LSE_CHUNK
        cw = nh * HD_A
        nhc = A_HG // nh
        nchunk = nh // A_LSE_CHUNK

        def seg_spec(seg, prev):
            if prev:
                return pl.BlockSpec((None, None, BLOCK, cw),
                                    lambda b, hc, mb, r, seg=seg: (b, r, jnp.maximum(mb - 1, 0), seg * nhc + hc))
            return pl.BlockSpec((None, None, BLOCK, cw), lambda b, hc, mb, r, seg=seg: (b, r, mb, seg * nhc + hc))

        o_spec = pl.BlockSpec((nh, BLOCK * dil, LANES), lambda b, hc, mb, r: (hc, b * nblk + mb, 0))
        l_spec = pl.BlockSpec((nchunk, BLOCK * dil, LANES), lambda b, hc, mb, r: (hc, b * nblk + mb, 0))
        args = [qkv, qkv]
        in_specs = [seg_spec(0, False), seg_spec(1, False)]
        if use_prev:
            args.append(qkv)
            in_specs.append(seg_spec(1, True))
        args.append(qkv)
        in_specs.append(seg_spec(2, False))
        if use_prev:
            args.append(qkv)
            in_specs.append(seg_spec(2, True))
        if has_in:
            args += [o_state, lse_state]
            in_specs += [o_spec, l_spec]
        if final:
            out_shape = [jax.ShapeDtypeStruct((m_rows, gw), BF16)]
            out_specs = [pl.BlockSpec((BLOCK, cw), lambda b, hc, mb, r: (b * nblk + mb, hc))]
        else:
            out_shape = [jax.ShapeDtypeStruct((A_HG, m_rows, LANES), F32),
                         jax.ShapeDtypeStruct((A_HG // A_LSE_CHUNK, m_rows, LANES), F32)]
            out_specs = [o_spec, l_spec]
        outs = pl.pallas_call(
            functools.partial(_attn_a_prompt_kernel, dil=dil, use_prev=use_prev, has_in=has_in, final=final,
                              scale=HD_A ** -0.5),
            out_shape=out_shape,
            grid=(nb, nhc, nblk, dil),
            in_specs=in_specs,
            out_specs=out_specs,
            compiler_params=_params("parallel", "parallel", "arbitrary", "arbitrary"),
            name=f"attn_a_prompt_g{gi}",
        )(*args)
        if final:
            return outs[0]
        o_state, lse_state = outs


def _attn_a_sample_kernel(q_ref, kn_ref, vn_ref, c1_ref, c2_ref, c3_ref, o_ref, *, t_len, scale):
    nh = A_HG
    cache_refs = (c1_ref, c2_ref, c3_ref)
    o_mix = [None] * t_len
    lse_mix = [None] * t_len
    for gi, (win, dil) in enumerate(A_CONFIGS):
        cref = cache_refs[gi]
        nkey = cref.shape[0]
        lb = nkey * dil
        t_sets = [list(range(t_len))] if dil == 1 else [[t] for t in range(t_len)]
        for ts in t_sets:
            nt = len(ts)
            res = (lb + ts[0]) % dil
            nrow = nt * nh
            qf = q_ref[ts[0]:ts[0] + nt, gi].reshape(nrow, HD_A).astype(BF16)
            kmat = cref[:, res, 0].reshape(nkey * nh, HD_A).astype(BF16)
            vmat = cref[:, res, 1].reshape(nkey * nh, HD_A).astype(BF16)
            s = lax.dot_general(qf, kmat, (((1,), (1,)), ((), ())), preferred_element_type=F32) * scale
            col = lax.broadcasted_iota(jnp.int32, s.shape, 1)
            rowi = lax.broadcasted_iota(jnp.int32, s.shape, 0)
            dist = lb + ts[0] + _div_pow2(rowi, nh) - (_div_pow2(col, nh) * dil + res)
            s = jnp.where((_mod_pow2(col, nh) == _mod_pow2(rowi, nh)) & (dist >= 0) & (dist <= win), s, NEG_INF)
            rowt = ts[0] + _div_pow2(lax.broadcasted_iota(jnp.int32, (nrow, 1), 0), nh)
            qf32 = qf.astype(F32)
            news = []
            for u in range(ts[-1] + 1):
                if not any((t - u) % dil == 0 and 0 <= t - u <= win for t in ts):
                    continue
                kn = jnp.tile(kn_ref[u, gi].astype(BF16).astype(F32), (nt, 1))
                ok = (rowt >= u) & (_mod_pow2(rowt - u, dil) == 0) & (rowt - u <= win)
                sn = jnp.where(ok, jnp.sum(qf32 * kn, axis=1, keepdims=True) * scale, NEG_INF)
                news.append((u, sn))
            m = jnp.max(s, axis=1, keepdims=True)
            for _, sn in news:
                m = jnp.maximum(m, sn)
            p = jnp.exp(s - m)
            l = jnp.sum(p, axis=1, keepdims=True)
            o = jnp.dot(p.astype(BF16), vmat, preferred_element_type=F32)
            for u, sn in news:
                pn = jnp.exp(sn - m)
                l = l + pn
                vn = jnp.tile(vn_ref[u, gi].astype(BF16).astype(F32), (nt, 1))
                o = o + pn.astype(BF16).astype(F32) * vn
            o = o / l
            lse = m + jnp.log(l)
            for a, t in enumerate(ts):
                o_t = o[a * nh:(a + 1) * nh]
                lse_t = lse[a * nh:(a + 1) * nh]
                if o_mix[t] is None:
                    o_mix[t], lse_mix[t] = o_t, lse_t
                else:
                    top = jnp.maximum(lse_mix[t], lse_t)
                    lse_new = top + jnp.log(jnp.exp(lse_mix[t] - top) + jnp.exp(lse_t - top))
                    o_mix[t] = jnp.exp(lse_mix[t] - lse_new) * o_mix[t] + jnp.exp(lse_t - lse_new) * o_t
                    lse_mix[t] = lse_new
    for t in range(t_len):
        o_ref[t] = o_mix[t]


def attn_a_sample(q, k_new, v_new, caches, *, t_len, nb):
    cache_args = []
    cache_specs = []
    for (win, dil), c in zip(A_CONFIGS, caches):
        lb = c.shape[1]
        assert lb % dil == 0 and (dil == 1 or t_len <= dil)
        nres = min(dil, t_len)
        cache_args.append(c.reshape(nb, lb // dil, dil, 2, A_HG, HD_A))
        cache_specs.append(pl.BlockSpec((None, lb // dil, nres, 2, A_HG, HD_A), lambda b: (b, 0, 0, 0, 0, 0)))
    row_spec = pl.BlockSpec((None, t_len, A_GROUPS, A_HG, HD_A), lambda b: (b, 0, 0, 0, 0))
    return pl.pallas_call(
        functools.partial(_attn_a_sample_kernel, t_len=t_len, scale=HD_A ** -0.5),
        out_shape=jax.ShapeDtypeStruct((nb, t_len, A_HG, HD_A), F32),
        grid=(nb,),
        in_specs=[row_spec, row_spec, row_spec] + cache_specs,
        out_specs=pl.BlockSpec((None, t_len, A_HG, HD_A), lambda b: (b, 0, 0, 0)),
        compiler_params=_params("parallel"),
        name="attn_a_sample",
    )(q, k_new, v_new, *cache_args)


def _attn_b_kernel(q_ref, kvc_ref, kvp_ref, sink_ref, o_ref, *, scale, tq, stack, first_block_has_prev):
    mb = pl.program_id(1)
    nrow = stack * tq
    kw = KV_B * HD_B
    kv = jnp.concatenate([kvp_ref[...], kvc_ref[...]], axis=0)
    nk = kv.shape[0]
    rowi = lax.broadcasted_iota(jnp.int32, (nrow, nk), 0)
    kj = lax.broadcasted_iota(jnp.int32, (nrow, nk), 1)
    dist = _mod_pow2(rowi, tq) + BLOCK - kj
    valid = (dist >= 0) & (dist <= WINDOW_B)
    if not first_block_has_prev:
        valid = valid & ((mb > 0) | (kj >= BLOCK))
    lane = lax.broadcasted_iota(jnp.int32, (nk, LANES), 1)
    lane_q = lax.broadcasted_iota(jnp.int32, (nrow, LANES), 1)
    pair_of_row = _div_pow2(lax.broadcasted_iota(jnp.int32, (nrow, 1), 0), tq)
    tiles_per_kvh = q_ref.shape[1] // LANES // KV_B
    for kvh in range(KV_B):
        pc = (kvh * HD_B // LANES) * LANES
        kpair = kv[:, pc:pc + LANES]
        vpair = kv[:, kw + pc:kw + pc + LANES]
        if (kvh * HD_B) % LANES == 0:
            k_lo = jnp.where(lane < HD_B, kpair, 0.0)
            v_lo = jnp.where(lane < HD_B, vpair, 0.0)
            k_hi = pltpu.roll(k_lo, HD_B, 1)
            v_hi = pltpu.roll(v_lo, HD_B, 1)
        else:
            k_hi = jnp.where(lane >= HD_B, kpair, 0.0)
            v_hi = jnp.where(lane >= HD_B, vpair, 0.0)
            k_lo = pltpu.roll(k_hi, HD_B, 1)
            v_lo = pltpu.roll(v_hi, HD_B, 1)
        k2 = jnp.concatenate([k_lo, k_hi], axis=0).astype(BF16)
        v2 = jnp.concatenate([v_lo, v_hi], axis=0).astype(BF16)
        for tile in range(tiles_per_kvh):
            col = (kvh * tiles_per_kvh + tile) * LANES
            q = q_ref[:, col:col + LANES]
            s = lax.dot_general(q, k2, (((1,), (1,)), ((), ())), preferred_element_type=F32) * scale
            ps = []
            inv = []
            for half in range(2):
                head0 = ((kvh * tiles_per_kvh + tile) * stack) * 2 + half
                sink = jnp.full((nrow, 1), sink_ref[head0], F32)
                for pr in range(1, stack):
                    sink = jnp.where(pair_of_row == pr, sink_ref[head0 + 2 * pr], sink)
                sh = jnp.where(valid, s[:, half * nk:(half + 1) * nk], NEG_INF)
                m = jnp.maximum(jnp.max(sh, axis=1, keepdims=True), sink)
                p = jnp.exp(sh - m)
                l = jnp.sum(p, axis=1, keepdims=True) + jnp.exp(sink - m)
                ps.append(p)
                inv.append(1.0 / l)
            p2 = jnp.concatenate(ps, axis=1).astype(BF16)
            o = jnp.dot(p2, v2, preferred_element_type=F32)
            o_ref[:, col:col + LANES] = (o * jnp.where(lane_q < HD_B, inv[0], inv[1])).astype(o_ref.dtype)


def attn_b(q, kv_cur, kv_prev, sinks, *, nb, nblk, tq, stack, first_block_has_prev):
    dq = q.shape[2]
    dkv = kv_cur.shape[2]
    if kv_prev is None:
        prev_arg = kv_cur
        prev_spec = pl.BlockSpec((None, BLOCK, dkv), lambda b, mb: (b, jnp.maximum(mb - 1, 0), 0))
    else:
        assert nblk == 1
        prev_arg = kv_prev
        prev_spec = pl.BlockSpec((None, BLOCK, dkv), lambda b, mb: (b, 0, 0))
    return pl.pallas_call(
        functools.partial(_attn_b_kernel, scale=HD_B ** -0.5, tq=tq, stack=stack,
                          first_block_has_prev=first_block_has_prev),
        out_shape=jax.ShapeDtypeStruct(q.shape, BF16),
        grid=(nb, nblk),
        in_specs=[pl.BlockSpec((None, stack * tq, dq), lambda b, mb: (b, mb, 0)),
                  pl.BlockSpec((None, BLOCK, dkv), lambda b, mb: (b, mb, 0)),
                  prev_spec,
                  pl.BlockSpec(memory_space=pltpu.SMEM)],
        out_specs=pl.BlockSpec((None, stack * tq, dq), lambda b, mb: (b, mb, 0)),
        compiler_params=_params("parallel", "arbitrary"),
        name="attn_b",
    )(q, kv_cur, prev_arg, sinks)


def _cast_pad_kernel(w_ref, o_ref, *, rows, cols, padded):
    tr, tc = w_ref.shape
    o_ref[...] = _cast_tile(w_ref[...], pl.program_id(1) * tr, pl.program_id(2) * tc, rows, cols, padded)


def cast_pad(w, rows_pad, cols_pad, layer=None):
    nl, rows, cols = w.shape
    tr, tc = CAST_TILE
    if layer is not None:
        return pl.pallas_call(
            functools.partial(_cast_pad_kernel, rows=rows, cols=cols, padded=(rows_pad, cols_pad) != (rows, cols)),
            out_shape=jax.ShapeDtypeStruct((1, rows_pad, cols_pad), BF16),
            grid=(1, rows_pad // tr, cols_pad // tc),
            in_specs=[pl.BlockSpec((None, tr, tc), lambda l, i, j: (layer, i, j))],
            out_specs=pl.BlockSpec((None, tr, tc), lambda l, i, j: (0, i, j)),
            compiler_params=_params("parallel", "parallel", "parallel"),
            name="cast_pad",
        )(w)
    return pl.pallas_call(
        functools.partial(_cast_pad_kernel, rows=rows, cols=cols, padded=(rows_pad, cols_pad) != (rows, cols)),
        out_shape=jax.ShapeDtypeStruct((nl, rows_pad, cols_pad), BF16),
        grid=(nl, rows_pad // tr, cols_pad // tc),
        in_specs=[pl.BlockSpec((None, tr, tc), lambda l, i, j: (l, i, j))],
        out_specs=pl.BlockSpec((None, tr, tc), lambda l, i, j: (l, i, j)),
        compiler_params=_params("parallel", "parallel", "parallel"),
        name="cast_pad",
    )(w)


def _pad_last(w, n):
    return jnp.pad(w, [(0, 0)] * (w.ndim - 1) + [(0, n - w.shape[-1])])


def kernel(x_prompt, x_sample, cache_a1, cache_a2, cache_a3, cache_b, state_conv, norm_attn, norm_ffn,
           w_qkv_a, w_o_a, norm_kv, w_kv_b, w_q_b, sinks_b, w_o_b, w_gate, w_up, conv_w, conv_b, w_down,
           norm_final):
    nbp, seq, d = x_prompt.shape
    nbs, t_len, _ = x_sample.shape
    depth = norm_attn.shape[0]
    n_a = w_qkv_a.shape[0]
    d_ff = w_gate.shape[2]
    f_pad = -(-d_ff // FF_ALIGN) * FF_ALIGN
    gw = A_HG * HD_A
    nq = A_GROUPS * gw
    kw = KV_B * HD_B
    mp = nbp * seq
    ms = nbs * t_len
    tm_p = 1024
    tiles_per_seq = seq // tm_p
    a_caches = (cache_a1, cache_a2, cache_a3)
    shift_a = HD_A // ROT_FRAC // 2
    shift_b = HD_B // ROT_FRAC // 2

    pos_p = jnp.arange(seq, dtype=jnp.int32)
    pos_s = jnp.repeat(PAST_LEN + jnp.arange(t_len, dtype=jnp.int32), nbs)
    tabs_a_p, tabs_a_s = rope_tables(pos_p, HD_A), rope_tables(pos_s, HD_A)
    tabs_b_p, tabs_b_s = rope_tables(pos_p, HD_B), rope_tables(pos_s, HD_B)

    hp = x_prompt.reshape(mp, d)
    hs = x_sample.transpose(1, 0, 2).reshape(ms, d)

    dq = w_q_b.shape[2]
    wqkv_a = cast_pad(w_qkv_a, d, 3 * nq)
    wo_a = cast_pad(w_o_a, gw, d)
    wkv_b = cast_pad(w_kv_b[None], d, 2 * kw)
    wq_b = cast_pad(w_q_b, d, dq)
    wo_b = cast_pad(w_o_b, dq, d)
    cw = _pad_last(conv_w, f_pad)
    cb = _pad_last(conv_b[:, None], f_pad)
    ffn_src = {"gate": (w_gate, d, f_pad), "up": (w_up, d, f_pad), "down": (w_down, f_pad, d)}
    ffn_w = {}

    def ride_arg(*candidates):
        for name, lyr in candidates:
            if lyr < depth and (name, lyr) not in ffn_w:
                return (name, lyr), (ffn_src[name][0], lyr) + ffn_src[name][1:]
        return None, None

    def ffn_weight(name, lyr):
        if (name, lyr) not in ffn_w:
            ffn_w[name, lyr] = cast_pad(*ffn_src[name], layer=lyr)
        return ffn_w[name, lyr]

    new_a_p = [[] for _ in range(A_GROUPS)]
    new_a_s = [[] for _ in range(A_GROUPS)]
    conv_p, conv_s = [], []
    kv_p = kv_s = new_b_s = None

    for layer in range(depth):
        if layer < n_a:
            wo, wo_l = wo_a, layer
            xp, = rmsnorm(hp, norm_attn[layer][None], BF16, 512)
            xs, = rmsnorm(hs, norm_attn[layer][None], BF16, ms)
            qkv_groups = []
            for gi, (win, dil) in enumerate(A_CONFIGS):
                ride_key, ride_w = ride_arg(("gate", layer), ("up", layer), ("down", layer))
                qkv_g, kvf_g, *cast = proj(xp, wqkv_a, layer, tabs_a_p, segs=[gi * gw, nq + gi * gw, 2 * nq + gi * gw],
                                           seg_cols=gw, tm=tm_p, tn=512, tiles_per_seq=tiles_per_seq,
                                           n_rope_segs=2, f32_from_seg=1, shift=shift_a, dil=dil, ride_w=ride_w)
                if cast:
                    ffn_w[ride_key] = cast[0][None]
                qkv_groups.append(qkv_g.reshape(nbp, dil, seq // dil, 3 * gw))
                new_a_p[gi].append(kvf_g.reshape(nbp, seq, 2, A_HG, HD_A)[:, -min(win, seq):])
            ap = attn_a_prompt(qkv_groups, nb=nbp, seq=seq)
            qkv_s = proj(xs, wqkv_a, layer, tabs_a_s, segs=[0, nq, 2 * nq], seg_cols=nq, tm=ms, tn=512, tiles_per_seq=1,
                         n_rope_segs=2, f32_from_seg=0, shift=shift_a, has_bf=False)
            qkv_s = qkv_s.reshape(t_len, nbs, 3, A_GROUPS, A_HG, HD_A).transpose(2, 1, 0, 3, 4, 5)
            as_ = attn_a_sample(qkv_s[0], qkv_s[1], qkv_s[2], [c[layer] for c in a_caches], t_len=t_len, nb=nbs)
            as_ = as_.transpose(1, 0, 2, 3).reshape(ms, gw).astype(BF16)
            for gi in range(A_GROUPS):
                kv_new = jnp.stack([qkv_s[1][:, :, gi], qkv_s[2][:, :, gi]], axis=2)
                new_a_s[gi].append(jnp.concatenate([a_caches[gi][layer], kv_new], axis=1)[:, t_len:])
        else:
            lb = layer - n_a
            wo, wo_l = wo_b, lb
            if layer == n_a:
                gains = jnp.stack([norm_attn[layer], norm_kv])
                xp, xkv_p = rmsnorm(hp, gains, BF16, 512)
                xs, xkv_s = rmsnorm(hs, gains, BF16, ms)
                kv_p = proj(xkv_p, wkv_b, 0, tabs_b_p, segs=[0, kw], seg_cols=kw, tm=tm_p, tn=kw,
                            tiles_per_seq=tiles_per_seq, n_rope_segs=1, f32_from_seg=0, shift=shift_b, has_bf=False)
                kv_s = proj(xkv_s, wkv_b, 0, tabs_b_s, segs=[0, kw], seg_cols=kw, tm=ms, tn=kw,
                            tiles_per_seq=1, n_rope_segs=1, f32_from_seg=0, shift=shift_b, has_bf=False)
                kv_s5 = kv_s.reshape(t_len, nbs, 2, KV_B, HD_B).transpose(1, 0, 2, 3, 4)
                new_b_s = jnp.concatenate([cache_b, kv_s5], axis=1)[:, t_len:]
            else:
                xp, = rmsnorm(hp, norm_attn[layer][None], BF16, 512)
                xs, = rmsnorm(hs, norm_attn[layer][None], BF16, ms)
            q_p = proj(xp, wq_b, lb, tabs_b_p, segs=[0], seg_cols=dq, tm=tm_p, tn=512, tiles_per_seq=tiles_per_seq,
                       n_rope_segs=1, f32_from_seg=None, shift=shift_b)
            q_s = proj(xs, wq_b, lb, tabs_b_s, segs=[0], seg_cols=dq, tm=ms, tn=512, tiles_per_seq=1,
                       n_rope_segs=1, f32_from_seg=None, shift=shift_b)
            ap = attn_b(q_p.reshape(nbp, seq, dq), kv_p.reshape(nbp, seq, 2 * kw), None, sinks_b[lb],
                        nb=nbp, nblk=seq // BLOCK, tq=BLOCK, stack=1, first_block_has_prev=False).reshape(mp, dq)
            npair = dq // LANES // KV_B
            q_sb = q_s.reshape(t_len, nbs, KV_B, npair, LANES).transpose(1, 3, 0, 2, 4)
            q_sb = q_sb.reshape(nbs, npair * t_len, KV_B * LANES)
            kv_sb = jnp.pad(kv_s.reshape(t_len, nbs, 2 * kw).transpose(1, 0, 2), ((0, 0), (0, BLOCK - t_len), (0, 0)))
            o_sb = attn_b(q_sb, kv_sb, cache_b.reshape(nbs, WINDOW_B, 2 * kw), sinks_b[lb],
                          nb=nbs, nblk=1, tq=t_len, stack=npair, first_block_has_prev=True)
            as_ = o_sb.reshape(nbs, npair, t_len, KV_B, LANES).transpose(2, 0, 3, 1, 4).reshape(ms, dq)
        hp = mm_res(ap, wo, wo_l, hp, tm=1024, tn=1024, tk=wo.shape[1])
        hs = mm_res(as_, wo, wo_l, hs, tm=ms, tn=1024, tk=wo.shape[1])

        xp, = rmsnorm(hp, norm_ffn[layer][None], BF16, 512)
        xs, = rmsnorm(hs, norm_ffn[layer][None], BF16, ms)
        wg, wu = ffn_weight("gate", layer), ffn_weight("up", layer)
        ride_key, ride_w = ride_arg(("gate", layer + 1), ("down", layer))
        act_p, tails, *cast = ffn_up_prompt(xp, wg, wu, 0, cw, cb, layer, tm=tm_p, tf=512,
                                            tiles_per_seq=tiles_per_seq, ride_w=ride_w)
        if cast:
            ffn_w[ride_key] = cast[0][None]
        prev_s = _pad_last(state_conv[layer].transpose(1, 0, 2).reshape((CONV_W - 1) * nbs, d_ff), f_pad)
        act_s, tail_s = ffn_up_sample(xs, wg, wu, 0, cw, cb, layer, prev_s, tf=1024, nb=nbs)
        conv_p.append(tails.reshape(nbp, tiles_per_seq, 8, f_pad)[:, -1, 8 - (CONV_W - 1):, :d_ff])
        conv_s.append(tail_s.reshape(CONV_W - 1, nbs, f_pad).transpose(1, 0, 2)[:, :, :d_ff])
        tk_down = f_pad // 4
        wd = ffn_weight("down", layer)
        ride_key, ride_w = ride_arg(("up", layer + 1))
        hp = mm_res(act_p, wd, 0, hp, tm=1024, tn=1024, tk=tk_down, ride_w=ride_w)
        if ride_w is not None:
            hp, cast_w = hp
            ffn_w[ride_key] = cast_w[None]
        hs = mm_res(act_s, wd, 0, hs, tm=ms, tn=1024, tk=tk_down)

    y_p, = rmsnorm(hp, norm_final[None], F32, 512)
    y_s, = rmsnorm(hs, norm_final[None], F32, ms)
    y_prompt = y_p.reshape(nbp, seq, d)
    y_sample = y_s.reshape(t_len, nbs, d).transpose(1, 0, 2)

    new_b_p = kv_p.reshape(nbp, seq, 2, KV_B, HD_B)[:, -min(WINDOW_B, seq):]
    outs = [y_prompt, y_sample]
    for gi in range(A_GROUPS):
        outs += [jnp.stack(new_a_p[gi], axis=0), jnp.stack(new_a_s[gi], axis=0)]
    outs += [new_b_p, new_b_s, jnp.stack(conv_p, axis=0), jnp.stack(conv_s, axis=0)]
    return tuple(outs)
```

```python
import functools

import jax
import jax.numpy as jnp
from jax import lax
from jax.experimental import pallas as pl
from jax.experimental.pallas import tpu as pltpu

F32 = jnp.float32
BF16 = jnp.bfloat16

HD_A = 128
A_HG = 16
A_CONFIGS = ((128, 1), (512, 4), (2048, 16))
A_GROUPS = len(A_CONFIGS)
A_LSE_CHUNK = 8
HD_B = 64
KV_B = 8
WINDOW_B = 128
PAST_LEN = 16384
CONV_W = 3
ROPE_THETA = 500000.0
ROT_FRAC = 4
NORM_EPS = 1e-5
BLOCK = 128
NEG_INF = -1e30
LANES = 128
FF_ALIGN = 1024
FFN_ROWS = 64
CAST_TILE = (512, 1024)
CAST_CHUNK = (128, 128)
SHIFT_ROWS = 256
SHIFT_ROWS_B = 512
B_SHIFT_SPLIT = 2
VMEM_LIMIT = 60 * 1024 * 1024


def _params(*sem):
    return pltpu.CompilerParams(dimension_semantics=sem, vmem_limit_bytes=VMEM_LIMIT)


def _div_pow2(x, n):
    assert n & (n - 1) == 0
    return x >> (n.bit_length() - 1)


def _mod_pow2(x, n):
    assert n & (n - 1) == 0
    return x & (n - 1)


def _cast_tile(x, row0, col0, rows, cols, padded):
    if padded:
        r = lax.broadcasted_iota(jnp.int32, x.shape, 0) + row0
        c = lax.broadcasted_iota(jnp.int32, x.shape, 1) + col0
        x = jnp.where((r < rows) & (c < cols), x, 0.0)
    return x.astype(BF16)


class _Ride:
    def __init__(self, w, layer, rows_pad, cols_pad, lin, nsteps):
        self.w, self.layer, self.lin = w, layer, lin
        _, self.rows, self.cols = w.shape
        self.rows_pad, self.cols_pad = rows_pad, cols_pad
        self.ncol = cols_pad // CAST_TILE[1]
        self.ntile = (rows_pad // CAST_TILE[0]) * self.ncol
        assert rows_pad % CAST_TILE[0] == 0 and cols_pad % CAST_TILE[1] == 0 and nsteps >= self.ntile

    def tile(self, step):
        t = jnp.minimum(step, self.ntile - 1)
        return t // self.ncol, t % self.ncol

    @property
    def in_spec(self):
        return pl.BlockSpec((None,) + CAST_TILE, lambda *g: (self.layer,) + self.tile(self.lin(*g)))

    @property
    def out_spec(self):
        return pl.BlockSpec(CAST_TILE, lambda *g: self.tile(self.lin(*g)))

    @property
    def out_shape(self):
        return jax.ShapeDtypeStruct((self.rows_pad, self.cols_pad), BF16)

    def cast(self, w_ref, o_ref, step):
        ti, tj = self.tile(step)
        for r0 in range(0, CAST_TILE[0], CAST_CHUNK[0]):
            for c0 in range(0, CAST_TILE[1], CAST_CHUNK[1]):
                sl = (slice(r0, r0 + CAST_CHUNK[0]), slice(c0, c0 + CAST_CHUNK[1]))
                o_ref[sl] = _cast_tile(w_ref[sl], ti * CAST_TILE[0] + r0, tj * CAST_TILE[1] + c0, self.rows,
                                       self.cols, (self.rows_pad, self.cols_pad) != (self.rows, self.cols))


class _ShiftRide:
    def __init__(self, cache, new, block_rows, lin, nsteps):
        self.cache, self.new, self.lin = cache, new, lin
        nb, self.ngroups, t = cache.shape[:3]
        self.bg = block_rows // t
        self.nblk = self.ngroups // self.bg
        self.count = nb * self.nblk
        self.tail = cache.shape[2:]
        assert block_rows % t == 0 and self.ngroups % self.bg == 0 and self.bg > 1 and nsteps >= self.count

    @staticmethod
    def fits(cache, block_rows, nsteps):
        nb, ngroups, t = cache.shape[:3]
        bg = block_rows // t
        return block_rows % t == 0 and bg > 1 and ngroups % bg == 0 and nb * (ngroups // bg) <= nsteps

    def block(self, step):
        gid = jnp.minimum(step, self.count - 1)
        return gid // self.nblk, gid % self.nblk

    def _zeros(self):
        return (0,) * len(self.tail)

    @property
    def in_specs(self):
        def src(*g):
            b, i = self.block(self.lin(*g))
            return (b, jnp.minimum(i * self.bg + 1, self.ngroups - self.bg)) + self._zeros()

        return [pl.BlockSpec((pl.Element(1), pl.Element(self.bg)) + tuple(pl.Element(d) for d in self.tail), src),
                pl.BlockSpec((None, 1) + self.tail, lambda *g: (self.block(self.lin(*g))[0], 0) + self._zeros())]

    @property
    def args(self):
        return [self.cache, self.new]

    @property
    def out_spec(self):
        return pl.BlockSpec((None, self.bg) + self.tail, lambda *g: self.block(self.lin(*g)) + self._zeros())

    @property
    def out_shape(self):
        return jax.ShapeDtypeStruct(self.cache.shape, self.cache.dtype)

    def copy(self, in_ref, out_ref):
        out_ref[...] = in_ref[0]

    def fix_last(self, in_ref, new_ref, out_ref, step):
        _, i = self.block(step)

        @pl.when(i == self.nblk - 1)
        def _():
            out_ref[0:self.bg - 1] = in_ref[0, 1:self.bg]
            out_ref[self.bg - 1:self.bg] = new_ref[...]


def _rmsnorm_kernel(h_ref, g_ref, *o_refs):
    x = h_ref[...]
    y = x * lax.rsqrt(jnp.mean(x * x, axis=-1, keepdims=True) + NORM_EPS)
    for n, o_ref in enumerate(o_refs):
        o_ref[...] = (y * g_ref[n:n + 1, :]).astype(o_ref.dtype)


def rmsnorm(h, gains, out_dtype, tm):
    m, d = h.shape
    n = gains.shape[0]
    return pl.pallas_call(
        _rmsnorm_kernel,
        out_shape=[jax.ShapeDtypeStruct((m, d), out_dtype)] * n,
        grid=(m // tm,),
        in_specs=[pl.BlockSpec((tm, d), lambda i: (i, 0)),
                  pl.BlockSpec((n, d), lambda i: (0, 0))],
        out_specs=[pl.BlockSpec((tm, d), lambda i: (i, 0))] * n,
        compiler_params=_params("parallel"),
        name="rmsnorm",
    )(h, gains)


def rope_tables(pos, hd):
    rot = hd // ROT_FRAC
    half = rot // 2
    inv = ROPE_THETA ** (-jnp.arange(half, dtype=F32) * 2.0 / rot)
    ang = pos.astype(F32)[:, None] * inv[None, :]
    cos, sin = jnp.cos(ang), jnp.sin(ang)
    s = pos.shape[0]
    ones = jnp.ones((s, hd - rot), F32)
    zrest = jnp.zeros((s, hd - rot), F32)
    zhalf = jnp.zeros((s, half), F32)
    reps = LANES // hd
    cos_t = jnp.tile(jnp.concatenate([cos, cos, ones], axis=1), (1, reps))
    sa_t = jnp.tile(jnp.concatenate([-sin, zhalf, zrest], axis=1), (1, reps))
    sb_t = jnp.tile(jnp.concatenate([zhalf, sin, zrest], axis=1), (1, reps))
    return (jnp.stack([cos_t, jnp.ones_like(cos_t)]), jnp.stack([sa_t, jnp.zeros_like(sa_t)]),
            jnp.stack([sb_t, jnp.zeros_like(sb_t)]))


def _proj_kernel(x_ref, w_ref, cos_ref, sa_ref, sb_ref, *refs, shift, dil, has_bf, has_f32, ride):
    refs = list(refs)
    ride_w_ref = refs.pop(0) if ride else None
    obf_ref = refs.pop(0) if has_bf else None
    of32_ref = refs.pop(0) if has_f32 else None
    ride_o_ref = refs.pop(0) if ride else None
    acc_refs = (refs.pop(0), refs.pop(0))
    slab_ref = refs.pop(0) if dil > 1 else None
    s = pl.program_id(0)
    tm, tn = acc_refs[0].shape

    @pl.when(s == 0)
    def _():
        acc_refs[1][...] = jnp.zeros((tm, tn), F32)

    def finish(acc_ref):
        for c in range(tn // LANES):
            sl = slice(c * LANES, (c + 1) * LANES)
            xh = acc_ref[:, sl]
            xh = (xh * cos_ref[...] + pltpu.roll(xh, LANES - shift, 1) * sa_ref[...]
                  + pltpu.roll(xh, shift, 1) * sb_ref[...])
            if has_f32:
                of32_ref[:, sl] = xh
            if not has_bf:
                continue
            if dil == 1:
                obf_ref[:, sl] = xh.astype(obf_ref.dtype)
            else:
                slab_ref[...] = xh
                for r in range(dil):
                    obf_ref[r, :, sl] = slab_ref[pl.ds(r, tm // dil, stride=dil), :].astype(obf_ref.dtype)

    for parity in range(2):
        @pl.when(s % 2 == parity)
        def _(parity=parity):
            if ride:
                ride.cast(ride_w_ref, ride_o_ref, s)
            acc_refs[parity][...] = jnp.dot(x_ref[...], w_ref[...], preferred_element_type=F32)
            finish(acc_refs[1 - parity])


def proj(xn, w, layer, tables, *, segs, seg_cols, tm, tn, tiles_per_seq, n_rope_segs, f32_from_seg, shift,
         dil=1, has_bf=True, ride_w=None):
    m, k = xn.shape
    per_seg = seg_cols // tn
    nj = len(segs) * per_seg
    ncols = len(segs) * seg_cols
    starts = [s // tn for s in segs]
    step = starts[1] - starts[0] if len(segs) > 1 else 0
    assert all(starts[a] == starts[0] + a * step for a in range(len(segs)))
    has_f32 = f32_from_seg is not None
    f32_from = f32_from_seg * per_seg if has_f32 else nj
    n_rope = n_rope_segs * per_seg
    ntiles = (m // tm) * nj

    def cur(s):
        t = jnp.minimum(s, ntiles - 1)
        return t // nj, t % nj

    def prev(s):
        t = jnp.maximum(s - 1, 0)
        return t // nj, t % nj

    def w_map(s):
        _, j = cur(s)
        return layer, 0, starts[0] + (j // per_seg) * step + j % per_seg

    def tab_map(s):
        i, j = prev(s)
        return jnp.where(j < n_rope, 0, 1), i % tiles_per_seq, 0

    def bf_map(s):
        i, j = prev(s)
        return (i, j) if dil == 1 else (i // tiles_per_seq, 0, i % tiles_per_seq, j)

    def f32_map(s):
        i, j = prev(s)
        return i, jnp.maximum(j - f32_from, 0)

    out_shape, out_specs = [], []
    if has_bf:
        if dil == 1:
            out_shape.append(jax.ShapeDtypeStruct((m, ncols), BF16))
            out_specs.append(pl.BlockSpec((tm, tn), bf_map))
        else:
            nseq = m // (tm * tiles_per_seq)
            out_shape.append(jax.ShapeDtypeStruct((nseq, dil, tm * tiles_per_seq // dil, ncols), BF16))
            out_specs.append(pl.BlockSpec((None, dil, tm // dil, tn), bf_map))
    if has_f32:
        out_shape.append(jax.ShapeDtypeStruct((m, ncols - f32_from * tn), F32))
        out_specs.append(pl.BlockSpec((tm, tn), f32_map))
    tab_spec = pl.BlockSpec((None, tm, LANES), tab_map)
    in_specs = [pl.BlockSpec((tm, k), lambda s: (cur(s)[0], 0)),
                pl.BlockSpec((None, k, tn), w_map),
                tab_spec, tab_spec, tab_spec]
    args = [xn, w, *tables]
    ride = None
    if ride_w is not None:
        ride = _Ride(*ride_w, lin=lambda s: s, nsteps=ntiles + 1)
        in_specs.append(ride.in_spec)
        args.append(ride.w)
        out_shape.append(ride.out_shape)
        out_specs.append(ride.out_spec)
    outs = pl.pallas_call(
        functools.partial(_proj_kernel, shift=shift, dil=dil, has_bf=has_bf, has_f32=has_f32, ride=ride),
        out_shape=out_shape,
        grid=(ntiles + 1,),
        in_specs=in_specs,
        out_specs=out_specs,
        scratch_shapes=[pltpu.VMEM((tm, tn), F32), pltpu.VMEM((tm, tn), F32)]
        + ([pltpu.VMEM((tm, LANES), F32)] if dil > 1 else []),
        compiler_params=_params("arbitrary"),
        name="proj",
    )(*args)
    return outs if len(outs) > 1 else outs[0]


def _mm_res_kernel(x_ref, w_ref, res_ref, *refs, ride):
    refs = list(refs)
    ride_w_ref = refs.pop(0) if ride else None
    o_ref = refs.pop(0)

    @pl.when(pl.program_id(2) == 0)
    def _():
        o_ref[...] = res_ref[...]

    o_ref[...] += jnp.dot(x_ref[...], w_ref[...], preferred_element_type=F32)
    if ride:
        ride.cast(ride_w_ref, refs.pop(0), ride.lin(pl.program_id(0), pl.program_id(1), pl.program_id(2)))


def mm_res(x, w, layer, res, *, tm, tn, tk, ride_w=None):
    m, kdim = x.shape
    n = w.shape[2]
    grid = (m // tm, n // tn, kdim // tk)
    in_specs = [pl.BlockSpec((tm, tk), lambda i, j, k: (i, k)),
                pl.BlockSpec((None, tk, tn), lambda i, j, k: (layer, k, j)),
                pl.BlockSpec((tm, tn), lambda i, j, k: (i, j))]
    args = [x, w, res]
    out_shape = [jax.ShapeDtypeStruct((m, n), F32)]
    out_specs = [pl.BlockSpec((tm, tn), lambda i, j, k: (i, j))]
    ride = None
    if ride_w is not None:
        ride = _Ride(*ride_w, lin=lambda i, j, k: (i * grid[1] + j) * grid[2] + k, nsteps=grid[0] * grid[1] * grid[2])
        in_specs.append(ride.in_spec)
        args.append(ride.w)
        out_shape.append(ride.out_shape)
        out_specs.append(ride.out_spec)
    outs = pl.pallas_call(
        functools.partial(_mm_res_kernel, ride=ride),
        out_shape=out_shape,
        grid=grid,
        in_specs=in_specs,
        out_specs=out_specs,
        compiler_params=_params("parallel" if ride is None else "arbitrary",
                                "parallel" if ride is None else "arbitrary", "arbitrary"),
        name="mm_res",
    )(*args)
    return outs if ride else outs[0]


def _silu(c):
    return c * jax.nn.sigmoid(c)


def _ffn_up_prompt_kernel(x_ref, wg_ref, wu_ref, cw_ref, cb_ref, *refs, nj, tiles_per_seq, ride):
    refs = list(refs)
    ride_w_ref = refs.pop(0) if ride else None
    act_ref, tail_ref = refs.pop(0), refs.pop(0)
    ride_o_ref = refs.pop(0) if ride else None
    g0_ref, u0_ref, g1_ref, u1_ref, carry_ref = refs
    s = pl.program_id(0)
    t_prev = jnp.maximum(s - 1, 0)
    ip = t_prev // nj
    jp = t_prev % nj
    tm, tf = g0_ref.shape

    @pl.when(s == 0)
    def _():
        g1_ref[...] = jnp.zeros((tm, tf), F32)
        u1_ref[...] = jnp.zeros((tm, tf), F32)

    @pl.when((ip % tiles_per_seq) == 0)
    def _():
        carry_ref[jp] = jnp.zeros(carry_ref.shape[1:], F32)

    def finish(g_ref, u_ref):
        for c0 in range(0, tf, LANES):
            sl = slice(c0, c0 + LANES)
            cw0, cw1, cw2, cbv = cw_ref[0:1, sl], cw_ref[1:2, sl], cw_ref[2:3, sl], cb_ref[:, sl]
            prev = carry_ref[jp, :, sl]
            for a in range(0, tm, FFN_ROWS):
                if a == 0:
                    ext = jnp.concatenate([prev, g_ref[0:FFN_ROWS, sl]], axis=0)
                else:
                    ext = g_ref[a - 8:a + FFN_ROWS, sl]
                c = cbv + cw0 * pltpu.roll(ext, 2, 0)[8:] + cw1 * pltpu.roll(ext, 1, 0)[8:] + cw2 * ext[8:]
                act_ref[a:a + FFN_ROWS, sl] = (_silu(c) * u_ref[a:a + FFN_ROWS, sl]).astype(act_ref.dtype)
            last = g_ref[tm - 8:tm, sl]
            carry_ref[jp, :, sl] = last
            tail_ref[:, sl] = last

    bufs = ((g0_ref, u0_ref), (g1_ref, u1_ref))
    for parity in range(2):
        @pl.when(s % 2 == parity)
        def _(parity=parity):
            x = x_ref[...]
            bufs[parity][0][...] = jnp.dot(x, wg_ref[...], preferred_element_type=F32)
            bufs[parity][1][...] = jnp.dot(x, wu_ref[...], preferred_element_type=F32)
            finish(*bufs[1 - parity])
            if ride:
                ride.cast(ride_w_ref, ride_o_ref, s)


def ffn_up_prompt(xn, wg, wu, wlayer, cw, cb, layer, *, tm, tf, tiles_per_seq, ride_w=None):
    m, d = xn.shape
    f = wg.shape[2]
    nj = f // tf
    ntiles = (m // tm) * nj

    def cur(s):
        t = jnp.minimum(s, ntiles - 1)
        return t // nj, t % nj

    def prev(s):
        t = jnp.maximum(s - 1, 0)
        return t // nj, t % nj

    buf = pltpu.VMEM((tm, tf), F32)
    in_specs = [pl.BlockSpec((tm, d), lambda s: (cur(s)[0], 0)),
                pl.BlockSpec((None, d, tf), lambda s: (wlayer, 0, cur(s)[1])),
                pl.BlockSpec((None, d, tf), lambda s: (wlayer, 0, cur(s)[1])),
                pl.BlockSpec((None, CONV_W, tf), lambda s: (layer, 0, prev(s)[1])),
                pl.BlockSpec((None, 1, tf), lambda s: (layer, 0, prev(s)[1]))]
    args = [xn, wg, wu, cw, cb]
    out_shape = [jax.ShapeDtypeStruct((m, f), BF16), jax.ShapeDtypeStruct((m // tm, 8, f), F32)]
    out_specs = [pl.BlockSpec((tm, tf), lambda s: prev(s)),
                 pl.BlockSpec((None, 8, tf), lambda s: (prev(s)[0], 0, prev(s)[1]))]
    ride = None
    if ride_w is not None:
        ride = _Ride(*ride_w, lin=lambda s: s, nsteps=ntiles + 1)
        in_specs.append(ride.in_spec)
        args.append(ride.w)
        out_shape.append(ride.out_shape)
        out_specs.append(ride.out_spec)
    return pl.pallas_call(
        functools.partial(_ffn_up_prompt_kernel, nj=nj, tiles_per_seq=tiles_per_seq, ride=ride),
        out_shape=out_shape,
        grid=(ntiles + 1,),
        in_specs=in_specs,
        out_specs=out_specs,
        scratch_shapes=[buf, buf, buf, buf, pltpu.VMEM((nj, 8, tf), F32)],
        compiler_params=_params("arbitrary"),
        name="ffn_up_prompt",
    )(*args)


def _ffn_up_sample_kernel(x_ref, wg_ref, wu_ref, cw_ref, cb_ref, prev_ref, act_ref, tail_ref, *, nb):
    x = x_ref[...]
    g = jnp.dot(x, wg_ref[...], preferred_element_type=F32)
    u = jnp.dot(x, wu_ref[...], preferred_element_type=F32)
    m = g.shape[0]
    gext = jnp.concatenate([prev_ref[...], g], axis=0)
    c = (cb_ref[...] + cw_ref[0:1, :] * gext[0:m] + cw_ref[1:2, :] * gext[nb:nb + m]
         + cw_ref[2:3, :] * gext[2 * nb:2 * nb + m])
    act_ref[...] = (_silu(c) * u).astype(act_ref.dtype)
    tail_ref[...] = gext[m:, :]


def ffn_up_sample(xn, wg, wu, wlayer, cw, cb, layer, prev, *, tf, nb):
    m, d = xn.shape
    f = wg.shape[2]
    return pl.pallas_call(
        functools.partial(_ffn_up_sample_kernel, nb=nb),
        out_shape=[jax.ShapeDtypeStruct((m, f), BF16),
                   jax.ShapeDtypeStruct((2 * nb, f), F32)],
        grid=(f // tf,),
        in_specs=[pl.BlockSpec((m, d), lambda j: (0, 0)),
                  pl.BlockSpec((None, d, tf), lambda j: (wlayer, 0, j)),
                  pl.BlockSpec((None, d, tf), lambda j: (wlayer, 0, j)),
                  pl.BlockSpec((None, CONV_W, tf), lambda j: (layer, 0, j)),
                  pl.BlockSpec((None, 1, tf), lambda j: (layer, 0, j)),
                  pl.BlockSpec((2 * nb, tf), lambda j: (0, j))],
        out_specs=[pl.BlockSpec((m, tf), lambda j: (0, j)),
                   pl.BlockSpec((2 * nb, tf), lambda j: (0, j))],
        compiler_params=_params("parallel"),
        name="ffn_up_sample",
    )(xn, wg, wu, cw, cb, prev)


def _attn_a_prompt_kernel(*refs, dil, use_prev, has_in, final, scale, shift):
    it = iter(refs)
    q_ref = next(it)
    kc_ref = next(it)
    kp_ref = next(it) if use_prev else None
    vc_ref = next(it)
    vp_ref = next(it) if use_prev else None
    oin_ref = next(it) if has_in else None
    lin_ref = next(it) if has_in else None
    if shift:
        shift_in_ref, shift_new_ref = next(it), next(it)
    oout_ref = next(it)
    lout_ref = None if final else next(it)

    mb = pl.program_id(2)
    r = pl.program_id(3)
    if shift:
        shift_out_ref = next(it)
        shift.copy(shift_in_ref, shift_out_ref)
    rows = pl.ds(r, BLOCK, stride=dil) if dil > 1 else slice(None)
    nkeys = 2 * BLOCK if use_prev else BLOCK
    qi = lax.broadcasted_iota(jnp.int32, (BLOCK, nkeys), 0)
    kj = lax.broadcasted_iota(jnp.int32, (BLOCK, nkeys), 1)
    if use_prev:
        dist = qi + BLOCK - kj
        valid = (dist >= 0) & (dist <= BLOCK) & ((mb > 0) | (kj >= BLOCK))
    else:
        valid = qi >= kj
    lane = lax.broadcasted_iota(jnp.int32, (BLOCK, LANES), 1)
    nh = q_ref.shape[1] // HD_A
    nchunk = nh // A_LSE_CHUNK
    lse_tiles = [jnp.zeros((BLOCK, LANES), F32) for _ in range(nchunk)]
    lse_in_tiles = [lin_ref[c, rows, :] for c in range(nchunk)] if has_in else None
    for h in range(nh):
        hc, hl = divmod(h, A_LSE_CHUNK)
        sl = slice(h * HD_A, (h + 1) * HD_A)
        q = q_ref[:, sl]
        if use_prev:
            k = jnp.concatenate([kp_ref[:, sl], kc_ref[:, sl]], axis=0)
            v = jnp.concatenate([vp_ref[:, sl], vc_ref[:, sl]], axis=0)
        else:
            k = kc_ref[:, sl]
            v = vc_ref[:, sl]
        s = lax.dot_general(q, k, (((1,), (1,)), ((), ())), preferred_element_type=F32) * scale
        s = jnp.where(valid, s, NEG_INF)
        m = jnp.max(s, axis=1, keepdims=True)
        p = jnp.exp(s - m)
        l = jnp.sum(p, axis=1, keepdims=True)
        o = jnp.dot(p.astype(BF16), v, preferred_element_type=F32) / l
        lse = m + jnp.log(l)
        if has_in:
            lse_in = lse_in_tiles[hc][:, hl:hl + 1]
            top = jnp.maximum(lse_in, lse)
            lse_new = top + jnp.log(jnp.exp(lse_in - top) + jnp.exp(lse - top))
            o = jnp.exp(lse_in - lse_new) * oin_ref[h, rows, :] + jnp.exp(lse - lse_new) * o
            lse = lse_new
        if final:
            oout_ref[:, sl] = o.astype(oout_ref.dtype)
        else:
            oout_ref[h, rows, :] = o
            lse_tiles[hc] = jnp.where(lane == hl, lse, lse_tiles[hc])
    if not final:
        for c in range(nchunk):
            lout_ref[c, rows, :] = lse_tiles[c]
    if shift:
        shift.fix_last(shift_in_ref, shift_new_ref, shift_out_ref,
                       shift.lin(pl.program_id(0), pl.program_id(1), mb, r))


def _attn_a_steps(gi, nb, seq):
    dil = A_CONFIGS[gi][1]
    nh = A_HG if BLOCK * dil <= 512 else A_LSE_CHUNK
    return nb * (A_HG // nh) * (seq // dil // BLOCK) * dil


def attn_a_prompt(qkv_groups, *, nb, seq, shifts=None):
    m_rows = nb * seq
    gw = A_HG * HD_A
    o_state = lse_state = None
    order = sorted(range(A_GROUPS), key=lambda g: -A_CONFIGS[g][1])
    shifted = {}
    for step, gi in enumerate(order):
        win, dil = A_CONFIGS[gi]
        assert win // dil == BLOCK and seq % (dil * BLOCK) == 0
        nblk = seq // dil // BLOCK
        use_prev = nblk > 1
        has_in = step > 0
        final = step == A_GROUPS - 1
        assert not final or dil == 1
        qkv = qkv_groups[gi]
        nh = A_HG if BLOCK * dil <= 512 else A_LSE_CHUNK
        cw = nh * HD_A
        nhc = A_HG // nh
        nchunk = nh // A_LSE_CHUNK

        def seg_spec(seg, prev):
            if prev:
                return pl.BlockSpec((None, None, BLOCK, cw),
                                    lambda b, hc, mb, r, seg=seg: (b, r, jnp.maximum(mb - 1, 0), seg * nhc + hc))
            return pl.BlockSpec((None, None, BLOCK, cw), lambda b, hc, mb, r, seg=seg: (b, r, mb, seg * nhc + hc))

        o_spec = pl.BlockSpec((nh, BLOCK * dil, LANES), lambda b, hc, mb, r: (hc, b * nblk + mb, 0))
        l_spec = pl.BlockSpec((nchunk, BLOCK * dil, LANES), lambda b, hc, mb, r: (hc, b * nblk + mb, 0))
        args = [qkv, qkv]
        in_specs = [seg_spec(0, False), seg_spec(1, False)]
        if use_prev:
            args.append(qkv)
            in_specs.append(seg_spec(1, True))
        args.append(qkv)
        in_specs.append(seg_spec(2, False))
        if use_prev:
            args.append(qkv)
            in_specs.append(seg_spec(2, True))
        if has_in:
            args += [o_state, lse_state]
            in_specs += [o_spec, l_spec]
        if final:
            out_shape = [jax.ShapeDtypeStruct((m_rows, gw), BF16)]
            out_specs = [pl.BlockSpec((BLOCK, cw), lambda b, hc, mb, r: (b * nblk + mb, hc))]
        else:
            out_shape = [jax.ShapeDtypeStruct((A_HG, m_rows, LANES), F32),
                         jax.ShapeDtypeStruct((A_HG // A_LSE_CHUNK, m_rows, LANES), F32)]
            out_specs = [o_spec, l_spec]
        grid = (nb, nhc, nblk, dil)
        ride = None
        if shifts and gi in shifts:
            ride = _ShiftRide(*shifts[gi], lin=lambda b, hc, mb, r, g=grid: ((b * g[1] + hc) * g[2] + mb) * g[3] + r,
                              nsteps=grid[0] * grid[1] * grid[2] * grid[3])
            args += ride.args
            in_specs += ride.in_specs
            out_shape.append(ride.out_shape)
            out_specs.append(ride.out_spec)
        outs = pl.pallas_call(
            functools.partial(_attn_a_prompt_kernel, dil=dil, use_prev=use_prev, has_in=has_in, final=final,
                              scale=HD_A ** -0.5, shift=ride),
            out_shape=out_shape,
            grid=grid,
            in_specs=in_specs,
            out_specs=out_specs,
            compiler_params=_params("parallel" if ride is None else "arbitrary",
                                    "parallel" if ride is None else "arbitrary", "arbitrary", "arbitrary"),
            name=f"attn_a_prompt_g{gi}",
        )(*args)
        if ride is not None:
            shifted[gi] = outs[-1]
            outs = outs[:-1]
        if final:
            return outs[0], shifted
        o_state, lse_state = outs


def _attn_a_sample_kernel(q_ref, kn_ref, vn_ref, c1_ref, c2_ref, c3_ref, o_ref, *, t_len, scale):
    nh = A_HG
    cache_refs = (c1_ref, c2_ref, c3_ref)
    o_mix = [None] * t_len
    lse_mix = [None] * t_len
    for gi, (win, dil) in enumerate(A_CONFIGS):
        cref = cache_refs[gi]
        nkey = cref.shape[0]
        lb = nkey * dil
        t_sets = [list(range(t_len))] if dil == 1 else [[t] for t in range(t_len)]
        for ts in t_sets:
            nt = len(ts)
            res = (lb + ts[0]) % dil
            nrow = nt * nh
            qf = q_ref[ts[0]:ts[0] + nt, gi].reshape(nrow, HD_A).astype(BF16)
            kmat = cref[:, res, 0].reshape(nkey * nh, HD_A).astype(BF16)
            vmat = cref[:, res, 1].reshape(nkey * nh, HD_A).astype(BF16)
            s = lax.dot_general(qf, kmat, (((1,), (1,)), ((), ())), preferred_element_type=F32) * scale
            col = lax.broadcasted_iota(jnp.int32, s.shape, 1)
            rowi = lax.broadcasted_iota(jnp.int32, s.shape, 0)
            dist = lb + ts[0] + _div_pow2(rowi, nh) - (_div_pow2(col, nh) * dil + res)
            s = jnp.where((_mod_pow2(col, nh) == _mod_pow2(rowi, nh)) & (dist >= 0) & (dist <= win), s, NEG_INF)
            rowt = ts[0] + _div_pow2(lax.broadcasted_iota(jnp.int32, (nrow, 1), 0), nh)
            qf32 = qf.astype(F32)
            news = []
            for u in range(ts[-1] + 1):
                if not any((t - u) % dil == 0 and 0 <= t - u <= win for t in ts):
                    continue
                kn = jnp.tile(kn_ref[u, gi].astype(BF16).astype(F32), (nt, 1))
                ok = (rowt >= u) & (_mod_pow2(rowt - u, dil) == 0) & (rowt - u <= win)
                sn = jnp.where(ok, jnp.sum(qf32 * kn, axis=1, keepdims=True) * scale, NEG_INF)
                news.append((u, sn))
            m = jnp.max(s, axis=1, keepdims=True)
            for _, sn in news:
                m = jnp.maximum(m, sn)
            p = jnp.exp(s - m)
            l = jnp.sum(p, axis=1, keepdims=True)
            o = jnp.dot(p.astype(BF16), vmat, preferred_element_type=F32)
            for u, sn in news:
                pn = jnp.exp(sn - m)
                l = l + pn
                vn = jnp.tile(vn_ref[u, gi].astype(BF16).astype(F32), (nt, 1))
                o = o + pn.astype(BF16).astype(F32) * vn
            o = o / l
            lse = m + jnp.log(l)
            for a, t in enumerate(ts):
                o_t = o[a * nh:(a + 1) * nh]
                lse_t = lse[a * nh:(a + 1) * nh]
                if o_mix[t] is None:
                    o_mix[t], lse_mix[t] = o_t, lse_t
                else:
                    top = jnp.maximum(lse_mix[t], lse_t)
                    lse_new = top + jnp.log(jnp.exp(lse_mix[t] - top) + jnp.exp(lse_t - top))
                    o_mix[t] = jnp.exp(lse_mix[t] - lse_new) * o_mix[t] + jnp.exp(lse_t - lse_new) * o_t
                    lse_mix[t] = lse_new
    for t in range(t_len):
        o_ref[t] = o_mix[t]


def attn_a_sample(q, k_new, v_new, caches, *, t_len, nb):
    cache_args = []
    cache_specs = []
    for (win, dil), c in zip(A_CONFIGS, caches):
        lb = c.shape[1]
        assert lb % dil == 0 and (dil == 1 or t_len <= dil)
        nres = min(dil, t_len)
        cache_args.append(c.reshape(nb, lb // dil, dil, 2, A_HG, HD_A))
        cache_specs.append(pl.BlockSpec((None, lb // dil, nres, 2, A_HG, HD_A), lambda b: (b, 0, 0, 0, 0, 0)))
    row_spec = pl.BlockSpec((None, t_len, A_GROUPS, A_HG, HD_A), lambda b: (b, 0, 0, 0, 0))
    return pl.pallas_call(
        functools.partial(_attn_a_sample_kernel, t_len=t_len, scale=HD_A ** -0.5),
        out_shape=jax.ShapeDtypeStruct((nb, t_len, A_HG, HD_A), F32),
        grid=(nb,),
        in_specs=[row_spec, row_spec, row_spec] + cache_specs,
        out_specs=pl.BlockSpec((None, t_len, A_HG, HD_A), lambda b: (b, 0, 0, 0)),
        compiler_params=_params("parallel"),
        name="attn_a_sample",
    )(q, k_new, v_new, *cache_args)


def _attn_b_kernel(q_ref, kc_ref, vc_ref, kp_ref, vp_ref, sink_ref, *refs, scale, tq, stack,
                   first_block_has_prev, shift):
    if shift:
        shift_in_ref, shift_new_ref, o_ref, shift_out_ref = refs
        shift.copy(shift_in_ref, shift_out_ref)
    else:
        o_ref, = refs
    mb = pl.program_id(1)
    part = pl.program_id(2)
    nrow = stack * tq
    kvs = kc_ref.shape[1] // HD_B
    k_all = jnp.concatenate([kp_ref[...], kc_ref[...]], axis=0)
    v_all = jnp.concatenate([vp_ref[...], vc_ref[...]], axis=0)
    nk = k_all.shape[0]
    rowi = lax.broadcasted_iota(jnp.int32, (nrow, nk), 0)
    kj = lax.broadcasted_iota(jnp.int32, (nrow, nk), 1)
    dist = _mod_pow2(rowi, tq) + BLOCK - kj
    valid = (dist >= 0) & (dist <= WINDOW_B)
    if not first_block_has_prev:
        valid = valid & ((mb > 0) | (kj >= BLOCK))
    lane = lax.broadcasted_iota(jnp.int32, (nk, LANES), 1)
    lane_q = lax.broadcasted_iota(jnp.int32, (nrow, LANES), 1)
    pair_of_row = _div_pow2(lax.broadcasted_iota(jnp.int32, (nrow, 1), 0), tq)
    tiles_per_kvh = q_ref.shape[1] // LANES // kvs
    for kvh in range(kvs):
        pc = (kvh * HD_B // LANES) * LANES
        kpair = k_all[:, pc:pc + LANES]
        vpair = v_all[:, pc:pc + LANES]
        if (kvh * HD_B) % LANES == 0:
            k_lo = jnp.where(lane < HD_B, kpair, 0.0)
            v_lo = jnp.where(lane < HD_B, vpair, 0.0)
            k_hi = pltpu.roll(k_lo, HD_B, 1)
            v_hi = pltpu.roll(v_lo, HD_B, 1)
        else:
            k_hi = jnp.where(lane >= HD_B, kpair, 0.0)
            v_hi = jnp.where(lane >= HD_B, vpair, 0.0)
            k_lo = pltpu.roll(k_hi, HD_B, 1)
            v_lo = pltpu.roll(v_hi, HD_B, 1)
        k2 = jnp.concatenate([k_lo, k_hi], axis=0).astype(BF16)
        v2 = jnp.concatenate([v_lo, v_hi], axis=0).astype(BF16)
        for tile in range(tiles_per_kvh):
            col = (kvh * tiles_per_kvh + tile) * LANES
            q = q_ref[:, col:col + LANES]
            s = lax.dot_general(q, k2, (((1,), (1,)), ((), ())), preferred_element_type=F32) * scale
            ps = []
            inv = []
            for half in range(2):
                head0 = (((part * kvs + kvh) * tiles_per_kvh + tile) * stack) * 2 + half
                sink = jnp.full((nrow, 1), sink_ref[head0], F32)
                for pr in range(1, stack):
                    sink = jnp.where(pair_of_row == pr, sink_ref[head0 + 2 * pr], sink)
                sh = jnp.where(valid, s[:, half * nk:(half + 1) * nk], NEG_INF)
                m = jnp.maximum(jnp.max(sh, axis=1, keepdims=True), sink)
                p = jnp.exp(sh - m)
                l = jnp.sum(p, axis=1, keepdims=True) + jnp.exp(sink - m)
                ps.append(p)
                inv.append(1.0 / l)
            p2 = jnp.concatenate(ps, axis=1).astype(BF16)
            o = jnp.dot(p2, v2, preferred_element_type=F32)
            o_ref[:, col:col + LANES] = (o * jnp.where(lane_q < HD_B, inv[0], inv[1])).astype(o_ref.dtype)
    if shift:
        shift.fix_last(shift_in_ref, shift_new_ref, shift_out_ref, shift.lin(pl.program_id(0), mb, part))


def attn_b(q, kv_cur, kv_prev, sinks, *, nb, nblk, tq, stack, first_block_has_prev, nsplit=1, shift=None):
    dq = q.shape[2] // nsplit
    wk = KV_B * HD_B // nsplit
    assert wk % LANES == 0 and dq % LANES == 0
    if kv_prev is None:
        prev_arg = kv_cur
        prev_row = lambda b, mb: (b, jnp.maximum(mb - 1, 0))
    else:
        assert nblk == 1
        prev_arg = kv_prev
        prev_row = lambda b, mb: (b, 0)
    in_specs = [pl.BlockSpec((None, stack * tq, dq), lambda b, mb, p: (b, mb, p)),
                pl.BlockSpec((None, BLOCK, wk), lambda b, mb, p: (b, mb, p)),
                pl.BlockSpec((None, BLOCK, wk), lambda b, mb, p: (b, mb, nsplit + p)),
                pl.BlockSpec((None, BLOCK, wk), lambda b, mb, p: prev_row(b, mb) + (p,)),
                pl.BlockSpec((None, BLOCK, wk), lambda b, mb, p: prev_row(b, mb) + (nsplit + p,)),
                pl.BlockSpec(memory_space=pltpu.SMEM)]
    args = [q, kv_cur, kv_cur, prev_arg, prev_arg, sinks]
    out_shape = [jax.ShapeDtypeStruct(q.shape, BF16)]
    out_specs = [pl.BlockSpec((None, stack * tq, dq), lambda b, mb, p: (b, mb, p))]
    ride = None
    if shift is not None:
        ride = _ShiftRide(*shift, lin=lambda b, mb, p: (b * nblk + mb) * nsplit + p, nsteps=nb * nblk * nsplit)
        args += ride.args
        in_specs += ride.in_specs
        out_shape.append(ride.out_shape)
        out_specs.append(ride.out_spec)
    outs = pl.pallas_call(
        functools.partial(_attn_b_kernel, scale=HD_B ** -0.5, tq=tq, stack=stack,
                          first_block_has_prev=first_block_has_prev, shift=ride),
        out_shape=out_shape,
        grid=(nb, nblk, nsplit),
        in_specs=in_specs,
        out_specs=out_specs,
        compiler_params=_params("parallel" if ride is None else "arbitrary", "arbitrary", "arbitrary"),
        name="attn_b",
    )(*args)
    return outs[0] if ride is None else outs


def _cast_pad_kernel(w_ref, o_ref, *, rows, cols, padded):
    tr, tc = w_ref.shape
    o_ref[...] = _cast_tile(w_ref[...], pl.program_id(1) * tr, pl.program_id(2) * tc, rows, cols, padded)


def cast_pad(w, rows_pad, cols_pad, layer=None):
    nl, rows, cols = w.shape
    tr, tc = CAST_TILE
    if layer is not None:
        return pl.pallas_call(
            functools.partial(_cast_pad_kernel, rows=rows, cols=cols, padded=(rows_pad, cols_pad) != (rows, cols)),
            out_shape=jax.ShapeDtypeStruct((1, rows_pad, cols_pad), BF16),
            grid=(1, rows_pad // tr, cols_pad // tc),
            in_specs=[pl.BlockSpec((None, tr, tc), lambda l, i, j: (layer, i, j))],
            out_specs=pl.BlockSpec((None, tr, tc), lambda l, i, j: (0, i, j)),
            compiler_params=_params("parallel", "parallel", "parallel"),
            name="cast_pad",
        )(w)
    return pl.pallas_call(
        functools.partial(_cast_pad_kernel, rows=rows, cols=cols, padded=(rows_pad, cols_pad) != (rows, cols)),
        out_shape=jax.ShapeDtypeStruct((nl, rows_pad, cols_pad), BF16),
        grid=(nl, rows_pad // tr, cols_pad // tc),
        in_specs=[pl.BlockSpec((None, tr, tc), lambda l, i, j: (l, i, j))],
        out_specs=pl.BlockSpec((None, tr, tc), lambda l, i, j: (l, i, j)),
        compiler_params=_params("parallel", "parallel", "parallel"),
        name="cast_pad",
    )(w)


def _pad_last(w, n):
    return jnp.pad(w, [(0, 0)] * (w.ndim - 1) + [(0, n - w.shape[-1])])


def kernel(x_prompt, x_sample, cache_a1, cache_a2, cache_a3, cache_b, state_conv, norm_attn, norm_ffn,
           w_qkv_a, w_o_a, norm_kv, w_kv_b, w_q_b, sinks_b, w_o_b, w_gate, w_up, conv_w, conv_b, w_down,
           norm_final):
    nbp, seq, d = x_prompt.shape
    nbs, t_len, _ = x_sample.shape
    depth = norm_attn.shape[0]
    n_a = w_qkv_a.shape[0]
    d_ff = w_gate.shape[2]
    f_pad = -(-d_ff // FF_ALIGN) * FF_ALIGN
    gw = A_HG * HD_A
    nq = A_GROUPS * gw
    kw = KV_B * HD_B
    mp = nbp * seq
    ms = nbs * t_len
    tm_p = 1024
    tiles_per_seq = seq // tm_p
    a_caches = (cache_a1, cache_a2, cache_a3)
    shift_a = HD_A // ROT_FRAC // 2
    shift_b = HD_B // ROT_FRAC // 2

    pos_p = jnp.arange(seq, dtype=jnp.int32)
    pos_s = jnp.repeat(PAST_LEN + jnp.arange(t_len, dtype=jnp.int32), nbs)
    tabs_a_p, tabs_a_s = rope_tables(pos_p, HD_A), rope_tables(pos_s, HD_A)
    tabs_b_p, tabs_b_s = rope_tables(pos_p, HD_B), rope_tables(pos_s, HD_B)

    hp = x_prompt.reshape(mp, d)
    hs = x_sample.transpose(1, 0, 2).reshape(ms, d)

    dq = w_q_b.shape[2]
    wqkv_a = cast_pad(w_qkv_a, d, 3 * nq)
    wo_a = cast_pad(w_o_a, gw, d)
    wkv_b = cast_pad(w_kv_b[None], d, 2 * kw)
    wq_b = cast_pad(w_q_b, d, dq)
    wo_b = cast_pad(w_o_b, dq, d)
    cw = _pad_last(conv_w, f_pad)
    cb = _pad_last(conv_b[:, None], f_pad)
    ffn_src = {"gate": (w_gate, d, f_pad), "up": (w_up, d, f_pad), "down": (w_down, f_pad, d)}
    ffn_w = {}

    def ride_arg(*candidates):
        for name, lyr in candidates:
            if lyr < depth and (name, lyr) not in ffn_w:
                return (name, lyr), (ffn_src[name][0], lyr) + ffn_src[name][1:]
        return None, None

    def ffn_weight(name, lyr):
        if (name, lyr) not in ffn_w:
            ffn_w[name, lyr] = cast_pad(*ffn_src[name], layer=lyr)
        return ffn_w[name, lyr]

    new_a_p = [[] for _ in range(A_GROUPS)]
    new_a_s = [[] for _ in range(A_GROUPS)]
    conv_p, conv_s = [], []
    kv_p = kv_s = new_b_s = None
    pending_shifts = []

    for layer in range(depth):
        if layer < n_a:
            wo, wo_l = wo_a, layer
            xp, = rmsnorm(hp, norm_attn[layer][None], BF16, 512)
            xs, = rmsnorm(hs, norm_attn[layer][None], BF16, ms)
            qkv_groups = []
            for gi, (win, dil) in enumerate(A_CONFIGS):
                ride_key, ride_w = ride_arg(("gate", layer), ("up", layer), ("down", layer))
                qkv_g, kvf_g, *cast = proj(xp, wqkv_a, layer, tabs_a_p, segs=[gi * gw, nq + gi * gw, 2 * nq + gi * gw],
                                           seg_cols=gw, tm=tm_p, tn=512, tiles_per_seq=tiles_per_seq,
                                           n_rope_segs=2, f32_from_seg=1, shift=shift_a, dil=dil, ride_w=ride_w)
                if cast:
                    ffn_w[ride_key] = cast[0][None]
                qkv_groups.append(qkv_g.reshape(nbp, dil, seq // dil, 3 * gw))
                new_a_p[gi].append(kvf_g.reshape(nbp, seq, 2, A_HG, HD_A)[:, -min(win, seq):])
            qkv_s = proj(xs, wqkv_a, layer, tabs_a_s, segs=[0, nq, 2 * nq], seg_cols=nq, tm=ms, tn=512, tiles_per_seq=1,
                         n_rope_segs=2, f32_from_seg=0, shift=shift_a, has_bf=False)
            qkv_s = qkv_s.reshape(t_len, nbs, 3, A_GROUPS, A_HG, HD_A).transpose(2, 1, 0, 3, 4, 5)
            as_ = attn_a_sample(qkv_s[0], qkv_s[1], qkv_s[2], [c[layer] for c in a_caches], t_len=t_len, nb=nbs)
            as_ = as_.transpose(1, 0, 2, 3).reshape(ms, gw).astype(BF16)
            views = {}
            for gi in range(A_GROUPS):
                cache = a_caches[gi][layer]
                kv_new = jnp.stack([qkv_s[1][:, :, gi], qkv_s[2][:, :, gi]], axis=2)
                rows = cache.shape[1]
                new_a_s[gi].append(None)
                if rows % t_len == 0:
                    views[gi] = (cache.reshape((nbs, rows // t_len, t_len) + cache.shape[2:]), kv_new[:, None])
                else:
                    new_a_s[gi][-1] = jnp.concatenate([cache, kv_new], axis=1)[:, t_len:]
            by_size = sorted(views, key=lambda g: -views[g][0].shape[1])
            b_steps = B_SHIFT_SPLIT * nbp * (seq // BLOCK)
            if by_size and layer == n_a - 1 and depth > n_a and _ShiftRide.fits(views[by_size[0]][0], SHIFT_ROWS_B, b_steps):
                big = by_size.pop(0)
                pending_shifts.append((big, layer) + views[big])
            shifts, owner = {}, {}
            for gi in by_size:
                for host in sorted(range(A_GROUPS), key=lambda g: A_CONFIGS[g][1]):
                    if host not in shifts and _ShiftRide.fits(views[gi][0], SHIFT_ROWS, _attn_a_steps(host, nbp, seq)):
                        shifts[host], owner[host] = views[gi] + (SHIFT_ROWS,), gi
                        break
                else:
                    new_a_s[gi][-1] = jnp.concatenate([a_caches[gi][layer], views[gi][1][:, 0]], axis=1)[:, t_len:]
            ap, shifted = attn_a_prompt(qkv_groups, nb=nbp, seq=seq, shifts=shifts)
            for host, arr in shifted.items():
                new_a_s[owner[host]][-1] = arr.reshape(a_caches[owner[host]][layer].shape)
        else:
            lb = layer - n_a
            wo, wo_l = wo_b, lb
            if layer == n_a:
                gains = jnp.stack([norm_attn[layer], norm_kv])
                xp, xkv_p = rmsnorm(hp, gains, BF16, 512)
                xs, xkv_s = rmsnorm(hs, gains, BF16, ms)
                kv_p = proj(xkv_p, wkv_b, 0, tabs_b_p, segs=[0, kw], seg_cols=kw, tm=tm_p, tn=kw,
                            tiles_per_seq=tiles_per_seq, n_rope_segs=1, f32_from_seg=0, shift=shift_b, has_bf=False)
                kv_s = proj(xkv_s, wkv_b, 0, tabs_b_s, segs=[0, kw], seg_cols=kw, tm=ms, tn=kw,
                            tiles_per_seq=1, n_rope_segs=1, f32_from_seg=0, shift=shift_b, has_bf=False)
                kv_s5 = kv_s.reshape(t_len, nbs, 2, KV_B, HD_B).transpose(1, 0, 2, 3, 4)
                new_b_s = jnp.concatenate([cache_b, kv_s5], axis=1)[:, t_len:]
            else:
                xp, = rmsnorm(hp, norm_attn[layer][None], BF16, 512)
                xs, = rmsnorm(hs, norm_attn[layer][None], BF16, ms)
            q_p = proj(xp, wq_b, lb, tabs_b_p, segs=[0], seg_cols=dq, tm=tm_p, tn=512, tiles_per_seq=tiles_per_seq,
                       n_rope_segs=1, f32_from_seg=None, shift=shift_b)
            q_s = proj(xs, wq_b, lb, tabs_b_s, segs=[0], seg_cols=dq, tm=ms, tn=512, tiles_per_seq=1,
                       n_rope_segs=1, f32_from_seg=None, shift=shift_b)
            host = pending_shifts.pop(0) if pending_shifts else None
            ap = attn_b(q_p.reshape(nbp, seq, dq), kv_p.reshape(nbp, seq, 2 * kw), None, sinks_b[lb],
                        nb=nbp, nblk=seq // BLOCK, tq=BLOCK, stack=1, first_block_has_prev=False,
                        nsplit=B_SHIFT_SPLIT if host else 1, shift=host[2:] + (SHIFT_ROWS_B,) if host else None)
            if host:
                ap, shifted = ap
                new_a_s[host[0]][host[1]] = shifted.reshape(a_caches[host[0]][host[1]].shape)
            ap = ap.reshape(mp, dq)
            npair = dq // LANES // KV_B
            q_sb = q_s.reshape(t_len, nbs, KV_B, npair, LANES).transpose(1, 3, 0, 2, 4)
            q_sb = q_sb.reshape(nbs, npair * t_len, KV_B * LANES)
            kv_sb = jnp.pad(kv_s.reshape(t_len, nbs, 2 * kw).transpose(1, 0, 2), ((0, 0), (0, BLOCK - t_len), (0, 0)))
            o_sb = attn_b(q_sb, kv_sb, cache_b.reshape(nbs, WINDOW_B, 2 * kw), sinks_b[lb],
                          nb=nbs, nblk=1, tq=t_len, stack=npair, first_block_has_prev=True)
            as_ = o_sb.reshape(nbs, npair, t_len, KV_B, LANES).transpose(2, 0, 3, 1, 4).reshape(ms, dq)
        hp = mm_res(ap, wo, wo_l, hp, tm=1024, tn=1024, tk=wo.shape[1])
        hs = mm_res(as_, wo, wo_l, hs, tm=ms, tn=1024, tk=wo.shape[1])

        xp, = rmsnorm(hp, norm_ffn[layer][None], BF16, 512)
        xs, = rmsnorm(hs, norm_ffn[layer][None], BF16, ms)
        wg, wu = ffn_weight("gate", layer), ffn_weight("up", layer)
        ride_key, ride_w = ride_arg(("gate", layer + 1), ("down", layer))
        act_p, tails, *cast = ffn_up_prompt(xp, wg, wu, 0, cw, cb, layer, tm=tm_p, tf=512,
                                            tiles_per_seq=tiles_per_seq, ride_w=ride_w)
        if cast:
            ffn_w[ride_key] = cast[0][None]
        prev_s = _pad_last(state_conv[layer].transpose(1, 0, 2).reshape((CONV_W - 1) * nbs, d_ff), f_pad)
        act_s, tail_s = ffn_up_sample(xs, wg, wu, 0, cw, cb, layer, prev_s, tf=1024, nb=nbs)
        conv_p.append(tails.reshape(nbp, tiles_per_seq, 8, f_pad)[:, -1, 8 - (CONV_W - 1):, :d_ff])
        conv_s.append(tail_s.reshape(CONV_W - 1, nbs, f_pad).transpose(1, 0, 2)[:, :, :d_ff])
        tk_down = f_pad // 4
        wd = ffn_weight("down", layer)
        ride_key, ride_w = ride_arg(("up", layer + 1))
        hp = mm_res(act_p, wd, 0, hp, tm=1024, tn=1024, tk=tk_down, ride_w=ride_w)
        if ride_w is not None:
            hp, cast_w = hp
            ffn_w[ride_key] = cast_w[None]
        hs = mm_res(act_s, wd, 0, hs, tm=ms, tn=1024, tk=tk_down)

    for gi, lyr, _, new in pending_shifts:
        new_a_s[gi][lyr] = jnp.concatenate([a_caches[gi][lyr], new[:, 0]], axis=1)[:, t_len:]

    y_p, = rmsnorm(hp, norm_final[None], F32, 512)
    y_s, = rmsnorm(hs, norm_final[None], F32, ms)
    y_prompt = y_p.reshape(nbp, seq, d)
    y_sample = y_s.reshape(t_len, nbs, d).transpose(1, 0, 2)

    new_b_p = kv_p.reshape(nbp, seq, 2, KV_B, HD_B)[:, -min(WINDOW_B, seq):]
    outs = [y_prompt, y_sample]
    for gi in range(A_GROUPS):
        outs += [jnp.stack(new_a_p[gi], axis=0), jnp.stack(new_a_s[gi], axis=0)]
    outs += [new_b_p, new_b_s, jnp.stack(conv_p, axis=0), jnp.stack(conv_s, axis=0)]
    return tuple(outs)
```

```python
import functools

import jax
import jax.numpy as jnp
from jax import lax
from jax.experimental import pallas as pl
from jax.experimental.pallas import tpu as pltpu

F32 = jnp.float32
BF16 = jnp.bfloat16

HD_A = 128
A_HG = 16
A_CONFIGS = ((128, 1), (512, 4), (2048, 16))
A_GROUPS = len(A_CONFIGS)
A_LSE_CHUNK = 8
HD_B = 64
KV_B = 8
WINDOW_B = 128
PAST_LEN = 16384
CONV_W = 3
ROPE_THETA = 500000.0
ROT_FRAC = 4
NORM_EPS = 1e-5
BLOCK = 128
NEG_INF = -1e30
LANES = 128
FF_ALIGN = 1024
FFN_ROWS = 64
CAST_TILE = (512, 1024)
CAST_CHUNK = (128, 128)
SHIFT_ROWS = 256
SHIFT_ROWS_B = 512
B_SHIFT_SPLIT = 2
VMEM_LIMIT = 60 * 1024 * 1024


def _params(*sem):
    return pltpu.CompilerParams(dimension_semantics=sem, vmem_limit_bytes=VMEM_LIMIT)


def _div_pow2(x, n):
    assert n & (n - 1) == 0
    return x >> (n.bit_length() - 1)


def _mod_pow2(x, n):
    assert n & (n - 1) == 0
    return x & (n - 1)


def _cast_tile(x, row0, col0, rows, cols, padded):
    if padded:
        r = lax.broadcasted_iota(jnp.int32, x.shape, 0) + row0
        c = lax.broadcasted_iota(jnp.int32, x.shape, 1) + col0
        x = jnp.where((r < rows) & (c < cols), x, 0.0)
    return x.astype(BF16)


class _Ride:
    def __init__(self, w, layer, rows_pad, cols_pad, lin, nsteps):
        self.w, self.layer, self.lin = w, layer, lin
        _, self.rows, self.cols = w.shape
        self.rows_pad, self.cols_pad = rows_pad, cols_pad
        self.ncol = cols_pad // CAST_TILE[1]
        self.ntile = (rows_pad // CAST_TILE[0]) * self.ncol
        assert rows_pad % CAST_TILE[0] == 0 and cols_pad % CAST_TILE[1] == 0 and nsteps >= self.ntile

    def tile(self, step):
        t = jnp.minimum(step, self.ntile - 1)
        return t // self.ncol, t % self.ncol

    @property
    def in_spec(self):
        return pl.BlockSpec((None,) + CAST_TILE, lambda *g: (self.layer,) + self.tile(self.lin(*g)))

    @property
    def out_spec(self):
        return pl.BlockSpec(CAST_TILE, lambda *g: self.tile(self.lin(*g)))

    @property
    def out_shape(self):
        return jax.ShapeDtypeStruct((self.rows_pad, self.cols_pad), BF16)

    def cast(self, w_ref, o_ref, step):
        ti, tj = self.tile(step)
        for r0 in range(0, CAST_TILE[0], CAST_CHUNK[0]):
            for c0 in range(0, CAST_TILE[1], CAST_CHUNK[1]):
                sl = (slice(r0, r0 + CAST_CHUNK[0]), slice(c0, c0 + CAST_CHUNK[1]))
                o_ref[sl] = _cast_tile(w_ref[sl], ti * CAST_TILE[0] + r0, tj * CAST_TILE[1] + c0, self.rows,
                                       self.cols, (self.rows_pad, self.cols_pad) != (self.rows, self.cols))


class _ShiftRide:
    def __init__(self, cache, new, block_rows, lin, nsteps):
        self.cache, self.new, self.lin = cache, new, lin
        nb, self.ngroups, t = cache.shape[:3]
        self.bg = block_rows // t
        self.nblk = self.ngroups // self.bg
        self.count = nb * self.nblk
        self.tail = cache.shape[2:]
        assert block_rows % t == 0 and self.ngroups % self.bg == 0 and self.bg > 1 and nsteps >= self.count

    @staticmethod
    def fits(cache, block_rows, nsteps):
        nb, ngroups, t = cache.shape[:3]
        bg = block_rows // t
        return block_rows % t == 0 and bg > 1 and ngroups % bg == 0 and nb * (ngroups // bg) <= nsteps

    def block(self, step):
        gid = jnp.minimum(step, self.count - 1)
        return gid // self.nblk, gid % self.nblk

    def _zeros(self):
        return (0,) * len(self.tail)

    @property
    def in_specs(self):
        def src(*g):
            b, i = self.block(self.lin(*g))
            return (b, jnp.minimum(i * self.bg + 1, self.ngroups - self.bg)) + self._zeros()

        return [pl.BlockSpec((pl.Element(1), pl.Element(self.bg)) + tuple(pl.Element(d) for d in self.tail), src),
                pl.BlockSpec((None, 1) + self.tail, lambda *g: (self.block(self.lin(*g))[0], 0) + self._zeros())]

    @property
    def args(self):
        return [self.cache, self.new]

    @property
    def out_spec(self):
        return pl.BlockSpec((None, self.bg) + self.tail, lambda *g: self.block(self.lin(*g)) + self._zeros())

    @property
    def out_shape(self):
        return jax.ShapeDtypeStruct(self.cache.shape, self.cache.dtype)

    def copy(self, in_ref, out_ref):
        out_ref[...] = in_ref[0]

    def fix_last(self, in_ref, new_ref, out_ref, step):
        _, i = self.block(step)

        @pl.when(i == self.nblk - 1)
        def _():
            out_ref[0:self.bg - 1] = in_ref[0, 1:self.bg]
            out_ref[self.bg - 1:self.bg] = new_ref[...]


def _rmsnorm_kernel(h_ref, g_ref, *o_refs):
    x = h_ref[...]
    y = x * lax.rsqrt(jnp.mean(x * x, axis=-1, keepdims=True) + NORM_EPS)
    for n, o_ref in enumerate(o_refs):
        o_ref[...] = (y * g_ref[n:n + 1, :]).astype(o_ref.dtype)


def rmsnorm(h, gains, out_dtype, tm):
    m, d = h.shape
    n = gains.shape[0]
    return pl.pallas_call(
        _rmsnorm_kernel,
        out_shape=[jax.ShapeDtypeStruct((m, d), out_dtype)] * n,
        grid=(m // tm,),
        in_specs=[pl.BlockSpec((tm, d), lambda i: (i, 0)),
                  pl.BlockSpec((n, d), lambda i: (0, 0))],
        out_specs=[pl.BlockSpec((tm, d), lambda i: (i, 0))] * n,
        compiler_params=_params("parallel"),
        name="rmsnorm",
    )(h, gains)


def rope_tables(pos, hd):
    rot = hd // ROT_FRAC
    half = rot // 2
    inv = ROPE_THETA ** (-jnp.arange(half, dtype=F32) * 2.0 / rot)
    ang = pos.astype(F32)[:, None] * inv[None, :]
    cos, sin = jnp.cos(ang), jnp.sin(ang)
    s = pos.shape[0]
    ones = jnp.ones((s, hd - rot), F32)
    zrest = jnp.zeros((s, hd - rot), F32)
    zhalf = jnp.zeros((s, half), F32)
    reps = LANES // hd
    cos_t = jnp.tile(jnp.concatenate([cos, cos, ones], axis=1), (1, reps))
    sa_t = jnp.tile(jnp.concatenate([-sin, zhalf, zrest], axis=1), (1, reps))
    sb_t = jnp.tile(jnp.concatenate([zhalf, sin, zrest], axis=1), (1, reps))
    return (jnp.stack([cos_t, jnp.ones_like(cos_t)]), jnp.stack([sa_t, jnp.zeros_like(sa_t)]),
            jnp.stack([sb_t, jnp.zeros_like(sb_t)]))


def _proj_kernel(x_ref, w_ref, cos_ref, sa_ref, sb_ref, *refs, shift, dil, has_bf, has_f32, ride):
    refs = list(refs)
    ride_w_ref = refs.pop(0) if ride else None
    obf_ref = refs.pop(0) if has_bf else None
    of32_ref = refs.pop(0) if has_f32 else None
    ride_o_ref = refs.pop(0) if ride else None
    acc_refs = (refs.pop(0), refs.pop(0))
    slab_ref = refs.pop(0) if dil > 1 else None
    s = pl.program_id(0)
    tm, tn = acc_refs[0].shape

    @pl.when(s == 0)
    def _():
        acc_refs[1][...] = jnp.zeros((tm, tn), F32)

    def finish(acc_ref):
        for c in range(tn // LANES):
            sl = slice(c * LANES, (c + 1) * LANES)
            xh = acc_ref[:, sl]
            xh = (xh * cos_ref[...] + pltpu.roll(xh, LANES - shift, 1) * sa_ref[...]
                  + pltpu.roll(xh, shift, 1) * sb_ref[...])
            if has_f32:
                of32_ref[:, sl] = xh
            if not has_bf:
                continue
            if dil == 1:
                obf_ref[:, sl] = xh.astype(obf_ref.dtype)
            else:
                slab_ref[...] = xh
                for r in range(dil):
                    obf_ref[r, :, sl] = slab_ref[pl.ds(r, tm // dil, stride=dil), :].astype(obf_ref.dtype)

    for parity in range(2):
        @pl.when(s % 2 == parity)
        def _(parity=parity):
            if ride:
                ride.cast(ride_w_ref, ride_o_ref, s)
            acc_refs[parity][...] = jnp.dot(x_ref[...], w_ref[...], preferred_element_type=F32)
            finish(acc_refs[1 - parity])


def proj(xn, w, layer, tables, *, segs, seg_cols, tm, tn, tiles_per_seq, n_rope_segs, f32_from_seg, shift,
         dil=1, has_bf=True, ride_w=None):
    m, k = xn.shape
    per_seg = seg_cols // tn
    nj = len(segs) * per_seg
    ncols = len(segs) * seg_cols
    starts = [s // tn for s in segs]
    step = starts[1] - starts[0] if len(segs) > 1 else 0
    assert all(starts[a] == starts[0] + a * step for a in range(len(segs)))
    has_f32 = f32_from_seg is not None
    f32_from = f32_from_seg * per_seg if has_f32 else nj
    n_rope = n_rope_segs * per_seg
    ntiles = (m // tm) * nj

    def cur(s):
        t = jnp.minimum(s, ntiles - 1)
        return t // nj, t % nj

    def prev(s):
        t = jnp.maximum(s - 1, 0)
        return t // nj, t % nj

    def w_map(s):
        _, j = cur(s)
        return layer, 0, starts[0] + (j // per_seg) * step + j % per_seg

    def tab_map(s):
        i, j = prev(s)
        return jnp.where(j < n_rope, 0, 1), i % tiles_per_seq, 0

    def bf_map(s):
        i, j = prev(s)
        return (i, j) if dil == 1 else (i // tiles_per_seq, 0, i % tiles_per_seq, j)

    def f32_map(s):
        i, j = prev(s)
        return i, jnp.maximum(j - f32_from, 0)

    out_shape, out_specs = [], []
    if has_bf:
        if dil == 1:
            out_shape.append(jax.ShapeDtypeStruct((m, ncols), BF16))
            out_specs.append(pl.BlockSpec((tm, tn), bf_map))
        else:
            nseq = m // (tm * tiles_per_seq)
            out_shape.append(jax.ShapeDtypeStruct((nseq, dil, tm * tiles_per_seq // dil, ncols), BF16))
            out_specs.append(pl.BlockSpec((None, dil, tm // dil, tn), bf_map))
    if has_f32:
        out_shape.append(jax.ShapeDtypeStruct((m, ncols - f32_from * tn), F32))
        out_specs.append(pl.BlockSpec((tm, tn), f32_map))
    tab_spec = pl.BlockSpec((None, tm, LANES), tab_map)
    in_specs = [pl.BlockSpec((tm, k), lambda s: (cur(s)[0], 0)),
                pl.BlockSpec((None, k, tn), w_map),
                tab_spec, tab_spec, tab_spec]
    args = [xn, w, *tables]
    ride = None
    if ride_w is not None:
        ride = _Ride(*ride_w, lin=lambda s: s, nsteps=ntiles + 1)
        in_specs.append(ride.in_spec)
        args.append(ride.w)
        out_shape.append(ride.out_shape)
        out_specs.append(ride.out_spec)
    outs = pl.pallas_call(
        functools.partial(_proj_kernel, shift=shift, dil=dil, has_bf=has_bf, has_f32=has_f32, ride=ride),
        out_shape=out_shape,
        grid=(ntiles + 1,),
        in_specs=in_specs,
        out_specs=out_specs,
        scratch_shapes=[pltpu.VMEM((tm, tn), F32), pltpu.VMEM((tm, tn), F32)]
        + ([pltpu.VMEM((tm, LANES), F32)] if dil > 1 else []),
        compiler_params=_params("arbitrary"),
        name="proj",
    )(*args)
    return outs if len(outs) > 1 else outs[0]


def _mm_res_kernel(x_ref, w_ref, res_ref, *refs, ride):
    refs = list(refs)
    ride_w_ref = refs.pop(0) if ride else None
    o_ref = refs.pop(0)

    @pl.when(pl.program_id(2) == 0)
    def _():
        o_ref[...] = res_ref[...]

    o_ref[...] += jnp.dot(x_ref[...], w_ref[...], preferred_element_type=F32)
    if ride:
        ride.cast(ride_w_ref, refs.pop(0), ride.lin(pl.program_id(0), pl.program_id(1), pl.program_id(2)))


def mm_res(x, w, layer, res, *, tm, tn, tk, ride_w=None):
    m, kdim = x.shape
    n = w.shape[2]
    grid = (m // tm, n // tn, kdim // tk)
    in_specs = [pl.BlockSpec((tm, tk), lambda i, j, k: (i, k)),
                pl.BlockSpec((None, tk, tn), lambda i, j, k: (layer, k, j)),
                pl.BlockSpec((tm, tn), lambda i, j, k: (i, j))]
    args = [x, w, res]
    out_shape = [jax.ShapeDtypeStruct((m, n), F32)]
    out_specs = [pl.BlockSpec((tm, tn), lambda i, j, k: (i, j))]
    ride = None
    if ride_w is not None:
        ride = _Ride(*ride_w, lin=lambda i, j, k: (i * grid[1] + j) * grid[2] + k, nsteps=grid[0] * grid[1] * grid[2])
        in_specs.append(ride.in_spec)
        args.append(ride.w)
        out_shape.append(ride.out_shape)
        out_specs.append(ride.out_spec)
    outs = pl.pallas_call(
        functools.partial(_mm_res_kernel, ride=ride),
        out_shape=out_shape,
        grid=grid,
        in_specs=in_specs,
        out_specs=out_specs,
        compiler_params=_params("parallel" if ride is None else "arbitrary",
                                "parallel" if ride is None else "arbitrary", "arbitrary"),
        name="mm_res",
    )(*args)
    return outs if ride else outs[0]


def _silu(c):
    return c * jax.nn.sigmoid(c)


def _ffn_up_prompt_kernel(x_ref, wg_ref, wu_ref, cw_ref, cb_ref, *refs, nj, tiles_per_seq, ride):
    refs = list(refs)
    ride_w_ref = refs.pop(0) if ride else None
    act_ref, tail_ref = refs.pop(0), refs.pop(0)
    ride_o_ref = refs.pop(0) if ride else None
    g0_ref, u0_ref, g1_ref, u1_ref, carry_ref = refs
    s = pl.program_id(0)
    t_prev = jnp.maximum(s - 1, 0)
    ip = t_prev // nj
    jp = t_prev % nj
    tm, tf = g0_ref.shape

    @pl.when(s == 0)
    def _():
        g1_ref[...] = jnp.zeros((tm, tf), F32)
        u1_ref[...] = jnp.zeros((tm, tf), F32)

    @pl.when((ip % tiles_per_seq) == 0)
    def _():
        carry_ref[jp] = jnp.zeros(carry_ref.shape[1:], F32)

    def finish(g_ref, u_ref):
        for c0 in range(0, tf, LANES):
            sl = slice(c0, c0 + LANES)
            cw0, cw1, cw2, cbv = cw_ref[0:1, sl], cw_ref[1:2, sl], cw_ref[2:3, sl], cb_ref[:, sl]
            prev = carry_ref[jp, :, sl]
            for a in range(0, tm, FFN_ROWS):
                if a == 0:
                    ext = jnp.concatenate([prev, g_ref[0:FFN_ROWS, sl]], axis=0)
                else:
                    ext = g_ref[a - 8:a + FFN_ROWS, sl]
                c = cbv + cw0 * pltpu.roll(ext, 2, 0)[8:] + cw1 * pltpu.roll(ext, 1, 0)[8:] + cw2 * ext[8:]
                act_ref[a:a + FFN_ROWS, sl] = (_silu(c) * u_ref[a:a + FFN_ROWS, sl]).astype(act_ref.dtype)
            last = g_ref[tm - 8:tm, sl]
            carry_ref[jp, :, sl] = last
            tail_ref[:, sl] = last

    bufs = ((g0_ref, u0_ref), (g1_ref, u1_ref))
    for parity in range(2):
        @pl.when(s % 2 == parity)
        def _(parity=parity):
            x = x_ref[...]
            bufs[parity][0][...] = jnp.dot(x, wg_ref[...], preferred_element_type=F32)
            bufs[parity][1][...] = jnp.dot(x, wu_ref[...], preferred_element_type=F32)
            finish(*bufs[1 - parity])
            if ride:
                ride.cast(ride_w_ref, ride_o_ref, s)


def ffn_up_prompt(xn, wg, wu, wlayer, cw, cb, layer, *, tm, tf, tiles_per_seq, ride_w=None):
    m, d = xn.shape
    f = wg.shape[2]
    nj = f // tf
    ntiles = (m // tm) * nj

    def cur(s):
        t = jnp.minimum(s, ntiles - 1)
        return t // nj, t % nj

    def prev(s):
        t = jnp.maximum(s - 1, 0)
        return t // nj, t % nj

    buf = pltpu.VMEM((tm, tf), F32)
    in_specs = [pl.BlockSpec((tm, d), lambda s: (cur(s)[0], 0)),
                pl.BlockSpec((None, d, tf), lambda s: (wlayer, 0, cur(s)[1])),
                pl.BlockSpec((None, d, tf), lambda s: (wlayer, 0, cur(s)[1])),
                pl.BlockSpec((None, CONV_W, tf), lambda s: (layer, 0, prev(s)[1])),
                pl.BlockSpec((None, 1, tf), lambda s: (layer, 0, prev(s)[1]))]
    args = [xn, wg, wu, cw, cb]
    out_shape = [jax.ShapeDtypeStruct((m, f), BF16), jax.ShapeDtypeStruct((m // tm, 8, f), F32)]
    out_specs = [pl.BlockSpec((tm, tf), lambda s: prev(s)),
                 pl.BlockSpec((None, 8, tf), lambda s: (prev(s)[0], 0, prev(s)[1]))]
    ride = None
    if ride_w is not None:
        ride = _Ride(*ride_w, lin=lambda s: s, nsteps=ntiles + 1)
        in_specs.append(ride.in_spec)
        args.append(ride.w)
        out_shape.append(ride.out_shape)
        out_specs.append(ride.out_spec)
    return pl.pallas_call(
        functools.partial(_ffn_up_prompt_kernel, nj=nj, tiles_per_seq=tiles_per_seq, ride=ride),
        out_shape=out_shape,
        grid=(ntiles + 1,),
        in_specs=in_specs,
        out_specs=out_specs,
        scratch_shapes=[buf, buf, buf, buf, pltpu.VMEM((nj, 8, tf), F32)],
        compiler_params=_params("arbitrary"),
        name="ffn_up_prompt",
    )(*args)


def _ffn_up_sample_kernel(x_ref, wg_ref, wu_ref, cw_ref, cb_ref, prev_ref, act_ref, tail_ref, *, nb):
    x = x_ref[...]
    g = jnp.dot(x, wg_ref[...], preferred_element_type=F32)
    u = jnp.dot(x, wu_ref[...], preferred_element_type=F32)
    m = g.shape[0]
    gext = jnp.concatenate([prev_ref[...], g], axis=0)
    c = (cb_ref[...] + cw_ref[0:1, :] * gext[0:m] + cw_ref[1:2, :] * gext[nb:nb + m]
         + cw_ref[2:3, :] * gext[2 * nb:2 * nb + m])
    act_ref[...] = (_silu(c) * u).astype(act_ref.dtype)
    tail_ref[...] = gext[m:, :]


def ffn_up_sample(xn, wg, wu, wlayer, cw, cb, layer, prev, *, tf, nb):
    m, d = xn.shape
    f = wg.shape[2]
    return pl.pallas_call(
        functools.partial(_ffn_up_sample_kernel, nb=nb),
        out_shape=[jax.ShapeDtypeStruct((m, f), BF16),
                   jax.ShapeDtypeStruct((2 * nb, f), F32)],
        grid=(f // tf,),
        in_specs=[pl.BlockSpec((m, d), lambda j: (0, 0)),
                  pl.BlockSpec((None, d, tf), lambda j: (wlayer, 0, j)),
                  pl.BlockSpec((None, d, tf), lambda j: (wlayer, 0, j)),
                  pl.BlockSpec((None, CONV_W, tf), lambda j: (layer, 0, j)),
                  pl.BlockSpec((None, 1, tf), lambda j: (layer, 0, j)),
                  pl.BlockSpec((2 * nb, tf), lambda j: (0, j))],
        out_specs=[pl.BlockSpec((m, tf), lambda j: (0, j)),
                   pl.BlockSpec((2 * nb, tf), lambda j: (0, j))],
        compiler_params=_params("parallel"),
        name="ffn_up_sample",
    )(xn, wg, wu, cw, cb, prev)


def _attn_a_prompt_kernel(*refs, dil, use_prev, has_in, final, scale, shift, ride):
    it = iter(refs)
    q_ref = next(it)
    kc_ref = next(it)
    kp_ref = next(it) if use_prev else None
    vc_ref = next(it)
    vp_ref = next(it) if use_prev else None
    oin_ref = next(it) if has_in else None
    lin_ref = next(it) if has_in else None
    if shift:
        shift_in_ref, shift_new_ref = next(it), next(it)
    ride_w_ref = next(it) if ride else None
    oout_ref = next(it)
    lout_ref = None if final else next(it)

    mb = pl.program_id(2)
    r = pl.program_id(3)
    if shift:
        shift_out_ref = next(it)
        shift.copy(shift_in_ref, shift_out_ref)
    if ride:
        ride.cast(ride_w_ref, next(it), ride.lin(pl.program_id(0), pl.program_id(1), mb, r))
    rows = pl.ds(r, BLOCK, stride=dil) if dil > 1 else slice(None)
    nkeys = 2 * BLOCK if use_prev else BLOCK
    qi = lax.broadcasted_iota(jnp.int32, (BLOCK, nkeys), 0)
    kj = lax.broadcasted_iota(jnp.int32, (BLOCK, nkeys), 1)
    if use_prev:
        dist = qi + BLOCK - kj
        valid = (dist >= 0) & (dist <= BLOCK) & ((mb > 0) | (kj >= BLOCK))
    else:
        valid = qi >= kj
    lane = lax.broadcasted_iota(jnp.int32, (BLOCK, LANES), 1)
    nh = q_ref.shape[1] // HD_A
    nchunk = nh // A_LSE_CHUNK
    lse_tiles = [jnp.zeros((BLOCK, LANES), F32) for _ in range(nchunk)]
    lse_in_tiles = [lin_ref[c, rows, :] for c in range(nchunk)] if has_in else None
    for h in range(nh):
        hc, hl = divmod(h, A_LSE_CHUNK)
        sl = slice(h * HD_A, (h + 1) * HD_A)
        q = q_ref[:, sl]
        if use_prev:
            k = jnp.concatenate([kp_ref[:, sl], kc_ref[:, sl]], axis=0)
            v = jnp.concatenate([vp_ref[:, sl], vc_ref[:, sl]], axis=0)
        else:
            k = kc_ref[:, sl]
            v = vc_ref[:, sl]
        s = lax.dot_general(q, k, (((1,), (1,)), ((), ())), preferred_element_type=F32) * scale
        s = jnp.where(valid, s, NEG_INF)
        m = jnp.max(s, axis=1, keepdims=True)
        p = jnp.exp(s - m)
        l = jnp.sum(p, axis=1, keepdims=True)
        o = jnp.dot(p.astype(BF16), v, preferred_element_type=F32) / l
        lse = m + jnp.log(l)
        if has_in:
            lse_in = lse_in_tiles[hc][:, hl:hl + 1]
            top = jnp.maximum(lse_in, lse)
            lse_new = top + jnp.log(jnp.exp(lse_in - top) + jnp.exp(lse - top))
            o = jnp.exp(lse_in - lse_new) * oin_ref[h, rows, :] + jnp.exp(lse - lse_new) * o
            lse = lse_new
        if final:
            oout_ref[:, sl] = o.astype(oout_ref.dtype)
        else:
            oout_ref[h, rows, :] = o
            lse_tiles[hc] = jnp.where(lane == hl, lse, lse_tiles[hc])
    if not final:
        for c in range(nchunk):
            lout_ref[c, rows, :] = lse_tiles[c]
    if shift:
        shift.fix_last(shift_in_ref, shift_new_ref, shift_out_ref,
                       shift.lin(pl.program_id(0), pl.program_id(1), mb, r))


def _attn_a_steps(gi, nb, seq):
    dil = A_CONFIGS[gi][1]
    nh = A_HG if BLOCK * dil <= 512 else A_LSE_CHUNK
    return nb * (A_HG // nh) * (seq // dil // BLOCK) * dil


def attn_a_prompt(qkv_groups, *, nb, seq, shifts=None, casts=None):
    m_rows = nb * seq
    gw = A_HG * HD_A
    o_state = lse_state = None
    order = sorted(range(A_GROUPS), key=lambda g: -A_CONFIGS[g][1])
    shifted, cast_out = {}, {}
    for step, gi in enumerate(order):
        win, dil = A_CONFIGS[gi]
        assert win // dil == BLOCK and seq % (dil * BLOCK) == 0
        nblk = seq // dil // BLOCK
        use_prev = nblk > 1
        has_in = step > 0
        final = step == A_GROUPS - 1
        assert not final or dil == 1
        qkv = qkv_groups[gi]
        nh = A_HG if BLOCK * dil <= 512 else A_LSE_CHUNK
        cw = nh * HD_A
        nhc = A_HG // nh
        nchunk = nh // A_LSE_CHUNK

        def seg_spec(seg, prev):
            if prev:
                return pl.BlockSpec((None, None, BLOCK, cw),
                                    lambda b, hc, mb, r, seg=seg: (b, r, jnp.maximum(mb - 1, 0), seg * nhc + hc))
            return pl.BlockSpec((None, None, BLOCK, cw), lambda b, hc, mb, r, seg=seg: (b, r, mb, seg * nhc + hc))

        o_spec = pl.BlockSpec((nh, BLOCK * dil, LANES), lambda b, hc, mb, r: (hc, b * nblk + mb, 0))
        l_spec = pl.BlockSpec((nchunk, BLOCK * dil, LANES), lambda b, hc, mb, r: (hc, b * nblk + mb, 0))
        args = [qkv, qkv]
        in_specs = [seg_spec(0, False), seg_spec(1, False)]
        if use_prev:
            args.append(qkv)
            in_specs.append(seg_spec(1, True))
        args.append(qkv)
        in_specs.append(seg_spec(2, False))
        if use_prev:
            args.append(qkv)
            in_specs.append(seg_spec(2, True))
        if has_in:
            args += [o_state, lse_state]
            in_specs += [o_spec, l_spec]
        if final:
            out_shape = [jax.ShapeDtypeStruct((m_rows, gw), BF16)]
            out_specs = [pl.BlockSpec((BLOCK, cw), lambda b, hc, mb, r: (b * nblk + mb, hc))]
        else:
            out_shape = [jax.ShapeDtypeStruct((A_HG, m_rows, LANES), F32),
                         jax.ShapeDtypeStruct((A_HG // A_LSE_CHUNK, m_rows, LANES), F32)]
            out_specs = [o_spec, l_spec]
        grid = (nb, nhc, nblk, dil)
        ride = None
        lin = lambda b, hc, mb, r, g=grid: ((b * g[1] + hc) * g[2] + mb) * g[3] + r
        nsteps = grid[0] * grid[1] * grid[2] * grid[3]
        if shifts and gi in shifts:
            ride = _ShiftRide(*shifts[gi], lin=lin, nsteps=nsteps)
            args += ride.args
            in_specs += ride.in_specs
        cast = None
        if casts and gi in casts:
            cast = _Ride(*casts[gi], lin=lin, nsteps=nsteps)
            args.append(cast.w)
            in_specs.append(cast.in_spec)
        if ride is not None:
            out_shape.append(ride.out_shape)
            out_specs.append(ride.out_spec)
        if cast is not None:
            out_shape.append(cast.out_shape)
            out_specs.append(cast.out_spec)
        seq_sem = "parallel" if ride is None and cast is None else "arbitrary"
        outs = pl.pallas_call(
            functools.partial(_attn_a_prompt_kernel, dil=dil, use_prev=use_prev, has_in=has_in, final=final,
                              scale=HD_A ** -0.5, shift=ride, ride=cast),
            out_shape=out_shape,
            grid=grid,
            in_specs=in_specs,
            out_specs=out_specs,
            compiler_params=_params(seq_sem, seq_sem, "arbitrary", "arbitrary"),
            name=f"attn_a_prompt_g{gi}",
        )(*args)
        if cast is not None:
            cast_out[gi] = outs[-1]
            outs = outs[:-1]
        if ride is not None:
            shifted[gi] = outs[-1]
            outs = outs[:-1]
        if final:
            return outs[0], shifted, cast_out
        o_state, lse_state = outs


def _attn_a_sample_kernel(q_ref, kn_ref, vn_ref, c1_ref, c2_ref, c3_ref, o_ref, *, t_len, scale):
    nh = A_HG
    cache_refs = (c1_ref, c2_ref, c3_ref)
    o_mix = [None] * t_len
    lse_mix = [None] * t_len
    for gi, (win, dil) in enumerate(A_CONFIGS):
        cref = cache_refs[gi]
        nkey = cref.shape[0]
        lb = nkey * dil
        t_sets = [list(range(t_len))] if dil == 1 else [[t] for t in range(t_len)]
        for ts in t_sets:
            nt = len(ts)
            res = (lb + ts[0]) % dil
            nrow = nt * nh
            qf = q_ref[ts[0]:ts[0] + nt, gi].reshape(nrow, HD_A).astype(BF16)
            kmat = cref[:, res, 0].reshape(nkey * nh, HD_A).astype(BF16)
            vmat = cref[:, res, 1].reshape(nkey * nh, HD_A).astype(BF16)
            s = lax.dot_general(qf, kmat, (((1,), (1,)), ((), ())), preferred_element_type=F32) * scale
            col = lax.broadcasted_iota(jnp.int32, s.shape, 1)
            rowi = lax.broadcasted_iota(jnp.int32, s.shape, 0)
            dist = lb + ts[0] + _div_pow2(rowi, nh) - (_div_pow2(col, nh) * dil + res)
            s = jnp.where((_mod_pow2(col, nh) == _mod_pow2(rowi, nh)) & (dist >= 0) & (dist <= win), s, NEG_INF)
            rowt = ts[0] + _div_pow2(lax.broadcasted_iota(jnp.int32, (nrow, 1), 0), nh)
            qf32 = qf.astype(F32)
            news = []
            for u in range(ts[-1] + 1):
                if not any((t - u) % dil == 0 and 0 <= t - u <= win for t in ts):
                    continue
                kn = jnp.tile(kn_ref[u, gi].astype(BF16).astype(F32), (nt, 1))
                ok = (rowt >= u) & (_mod_pow2(rowt - u, dil) == 0) & (rowt - u <= win)
                sn = jnp.where(ok, jnp.sum(qf32 * kn, axis=1, keepdims=True) * scale, NEG_INF)
                news.append((u, sn))
            m = jnp.max(s, axis=1, keepdims=True)
            for _, sn in news:
                m = jnp.maximum(m, sn)
            p = jnp.exp(s - m)
            l = jnp.sum(p, axis=1, keepdims=True)
            o = jnp.dot(p.astype(BF16), vmat, preferred_element_type=F32)
            for u, sn in news:
                pn = jnp.exp(sn - m)
                l = l + pn
                vn = jnp.tile(vn_ref[u, gi].astype(BF16).astype(F32), (nt, 1))
                o = o + pn.astype(BF16).astype(F32) * vn
            o = o / l
            lse = m + jnp.log(l)
            for a, t in enumerate(ts):
                o_t = o[a * nh:(a + 1) * nh]
                lse_t = lse[a * nh:(a + 1) * nh]
                if o_mix[t] is None:
                    o_mix[t], lse_mix[t] = o_t, lse_t
                else:
                    top = jnp.maximum(lse_mix[t], lse_t)
                    lse_new = top + jnp.log(jnp.exp(lse_mix[t] - top) + jnp.exp(lse_t - top))
                    o_mix[t] = jnp.exp(lse_mix[t] - lse_new) * o_mix[t] + jnp.exp(lse_t - lse_new) * o_t
                    lse_mix[t] = lse_new
    for t in range(t_len):
        o_ref[t] = o_mix[t]


def attn_a_sample(q, k_new, v_new, caches, *, t_len, nb):
    cache_args = []
    cache_specs = []
    for (win, dil), c in zip(A_CONFIGS, caches):
        lb = c.shape[1]
        assert lb % dil == 0 and (dil == 1 or t_len <= dil)
        nres = min(dil, t_len)
        cache_args.append(c.reshape(nb, lb // dil, dil, 2, A_HG, HD_A))
        cache_specs.append(pl.BlockSpec((None, lb // dil, nres, 2, A_HG, HD_A), lambda b: (b, 0, 0, 0, 0, 0)))
    row_spec = pl.BlockSpec((None, t_len, A_GROUPS, A_HG, HD_A), lambda b: (b, 0, 0, 0, 0))
    return pl.pallas_call(
        functools.partial(_attn_a_sample_kernel, t_len=t_len, scale=HD_A ** -0.5),
        out_shape=jax.ShapeDtypeStruct((nb, t_len, A_HG, HD_A), F32),
        grid=(nb,),
        in_specs=[row_spec, row_spec, row_spec] + cache_specs,
        out_specs=pl.BlockSpec((None, t_len, A_HG, HD_A), lambda b: (b, 0, 0, 0)),
        compiler_params=_params("parallel"),
        name="attn_a_sample",
    )(q, k_new, v_new, *cache_args)


def _attn_b_kernel(q_ref, kc_ref, vc_ref, kp_ref, vp_ref, sink_ref, *refs, scale, tq, stack,
                   first_block_has_prev, shift, ride):
    refs = list(refs)
    if shift:
        shift_in_ref, shift_new_ref = refs.pop(0), refs.pop(0)
    ride_w_ref = refs.pop(0) if ride else None
    o_ref = refs.pop(0)
    mb = pl.program_id(1)
    part = pl.program_id(2)
    if shift:
        shift_out_ref = refs.pop(0)
        shift.copy(shift_in_ref, shift_out_ref)
    if ride:
        ride.cast(ride_w_ref, refs.pop(0), ride.lin(pl.program_id(0), mb, part))
    nrow = stack * tq
    kvs = kc_ref.shape[1] // HD_B
    k_all = jnp.concatenate([kp_ref[...], kc_ref[...]], axis=0)
    v_all = jnp.concatenate([vp_ref[...], vc_ref[...]], axis=0)
    nk = k_all.shape[0]
    rowi = lax.broadcasted_iota(jnp.int32, (nrow, nk), 0)
    kj = lax.broadcasted_iota(jnp.int32, (nrow, nk), 1)
    dist = _mod_pow2(rowi, tq) + BLOCK - kj
    valid = (dist >= 0) & (dist <= WINDOW_B)
    if not first_block_has_prev:
        valid = valid & ((mb > 0) | (kj >= BLOCK))
    lane = lax.broadcasted_iota(jnp.int32, (nk, LANES), 1)
    lane_q = lax.broadcasted_iota(jnp.int32, (nrow, LANES), 1)
    pair_of_row = _div_pow2(lax.broadcasted_iota(jnp.int32, (nrow, 1), 0), tq)
    tiles_per_kvh = q_ref.shape[1] // LANES // kvs
    for kvh in range(kvs):
        pc = (kvh * HD_B // LANES) * LANES
        kpair = k_all[:, pc:pc + LANES]
        vpair = v_all[:, pc:pc + LANES]
        if (kvh * HD_B) % LANES == 0:
            k_lo = jnp.where(lane < HD_B, kpair, 0.0)
            v_lo = jnp.where(lane < HD_B, vpair, 0.0)
            k_hi = pltpu.roll(k_lo, HD_B, 1)
            v_hi = pltpu.roll(v_lo, HD_B, 1)
        else:
            k_hi = jnp.where(lane >= HD_B, kpair, 0.0)
            v_hi = jnp.where(lane >= HD_B, vpair, 0.0)
            k_lo = pltpu.roll(k_hi, HD_B, 1)
            v_lo = pltpu.roll(v_hi, HD_B, 1)
        k2 = jnp.concatenate([k_lo, k_hi], axis=0).astype(BF16)
        v2 = jnp.concatenate([v_lo, v_hi], axis=0).astype(BF16)
        for tile in range(tiles_per_kvh):
            col = (kvh * tiles_per_kvh + tile) * LANES
            q = q_ref[:, col:col + LANES]
            s = lax.dot_general(q, k2, (((1,), (1,)), ((), ())), preferred_element_type=F32) * scale
            ps = []
            inv = []
            for half in range(2):
                head0 = (((part * kvs + kvh) * tiles_per_kvh + tile) * stack) * 2 + half
                sink = jnp.full((nrow, 1), sink_ref[head0], F32)
                for pr in range(1, stack):
                    sink = jnp.where(pair_of_row == pr, sink_ref[head0 + 2 * pr], sink)
                sh = jnp.where(valid, s[:, half * nk:(half + 1) * nk], NEG_INF)
                m = jnp.maximum(jnp.max(sh, axis=1, keepdims=True), sink)
                p = jnp.exp(sh - m)
                l = jnp.sum(p, axis=1, keepdims=True) + jnp.exp(sink - m)
                ps.append(p)
                inv.append(1.0 / l)
            p2 = jnp.concatenate(ps, axis=1).astype(BF16)
            o = jnp.dot(p2, v2, preferred_element_type=F32)
            o_ref[:, col:col + LANES] = (o * jnp.where(lane_q < HD_B, inv[0], inv[1])).astype(o_ref.dtype)
    if shift:
        shift.fix_last(shift_in_ref, shift_new_ref, shift_out_ref, shift.lin(pl.program_id(0), mb, part))


def attn_b(q, kv_cur, kv_prev, sinks, *, nb, nblk, tq, stack, first_block_has_prev, nsplit=1, shift=None,
           cast_w=None):
    dq = q.shape[2] // nsplit
    wk = KV_B * HD_B // nsplit
    assert wk % LANES == 0 and dq % LANES == 0
    if kv_prev is None:
        prev_arg = kv_cur
        prev_row = lambda b, mb: (b, jnp.maximum(mb - 1, 0))
    else:
        assert nblk == 1
        prev_arg = kv_prev
        prev_row = lambda b, mb: (b, 0)
    in_specs = [pl.BlockSpec((None, stack * tq, dq), lambda b, mb, p: (b, mb, p)),
                pl.BlockSpec((None, BLOCK, wk), lambda b, mb, p: (b, mb, p)),
                pl.BlockSpec((None, BLOCK, wk), lambda b, mb, p: (b, mb, nsplit + p)),
                pl.BlockSpec((None, BLOCK, wk), lambda b, mb, p: prev_row(b, mb) + (p,)),
                pl.BlockSpec((None, BLOCK, wk), lambda b, mb, p: prev_row(b, mb) + (nsplit + p,)),
                pl.BlockSpec(memory_space=pltpu.SMEM)]
    args = [q, kv_cur, kv_cur, prev_arg, prev_arg, sinks]
    out_shape = [jax.ShapeDtypeStruct(q.shape, BF16)]
    out_specs = [pl.BlockSpec((None, stack * tq, dq), lambda b, mb, p: (b, mb, p))]
    lin = lambda b, mb, p: (b * nblk + mb) * nsplit + p
    ride = cast = None
    if shift is not None:
        ride = _ShiftRide(*shift, lin=lin, nsteps=nb * nblk * nsplit)
        args += ride.args
        in_specs += ride.in_specs
        out_shape.append(ride.out_shape)
        out_specs.append(ride.out_spec)
    if cast_w is not None:
        cast = _Ride(*cast_w, lin=lin, nsteps=nb * nblk * nsplit)
        args.append(cast.w)
        in_specs.append(cast.in_spec)
        out_shape.append(cast.out_shape)
        out_specs.append(cast.out_spec)
    outs = pl.pallas_call(
        functools.partial(_attn_b_kernel, scale=HD_B ** -0.5, tq=tq, stack=stack,
                          first_block_has_prev=first_block_has_prev, shift=ride, ride=cast),
        out_shape=out_shape,
        grid=(nb, nblk, nsplit),
        in_specs=in_specs,
        out_specs=out_specs,
        compiler_params=_params("parallel" if ride is None and cast is None else "arbitrary", "arbitrary",
                                "arbitrary"),
        name="attn_b",
    )(*args)
    return outs[0] if len(outs) == 1 else outs


def _cast_pad_kernel(w_ref, o_ref, *, rows, cols, padded):
    tr, tc = w_ref.shape
    o_ref[...] = _cast_tile(w_ref[...], pl.program_id(1) * tr, pl.program_id(2) * tc, rows, cols, padded)


def cast_pad(w, rows_pad, cols_pad, layer=None):
    nl, rows, cols = w.shape
    tr, tc = CAST_TILE
    if layer is not None:
        return pl.pallas_call(
            functools.partial(_cast_pad_kernel, rows=rows, cols=cols, padded=(rows_pad, cols_pad) != (rows, cols)),
            out_shape=jax.ShapeDtypeStruct((1, rows_pad, cols_pad), BF16),
            grid=(1, rows_pad // tr, cols_pad // tc),
            in_specs=[pl.BlockSpec((None, tr, tc), lambda l, i, j: (layer, i, j))],
            out_specs=pl.BlockSpec((None, tr, tc), lambda l, i, j: (0, i, j)),
            compiler_params=_params("parallel", "parallel", "parallel"),
            name="cast_pad",
        )(w)
    return pl.pallas_call(
        functools.partial(_cast_pad_kernel, rows=rows, cols=cols, padded=(rows_pad, cols_pad) != (rows, cols)),
        out_shape=jax.ShapeDtypeStruct((nl, rows_pad, cols_pad), BF16),
        grid=(nl, rows_pad // tr, cols_pad // tc),
        in_specs=[pl.BlockSpec((None, tr, tc), lambda l, i, j: (l, i, j))],
        out_specs=pl.BlockSpec((None, tr, tc), lambda l, i, j: (l, i, j)),
        compiler_params=_params("parallel", "parallel", "parallel"),
        name="cast_pad",
    )(w)


def _pad_last(w, n):
    return jnp.pad(w, [(0, 0)] * (w.ndim - 1) + [(0, n - w.shape[-1])])


def kernel(x_prompt, x_sample, cache_a1, cache_a2, cache_a3, cache_b, state_conv, norm_attn, norm_ffn,
           w_qkv_a, w_o_a, norm_kv, w_kv_b, w_q_b, sinks_b, w_o_b, w_gate, w_up, conv_w, conv_b, w_down,
           norm_final):
    nbp, seq, d = x_prompt.shape
    nbs, t_len, _ = x_sample.shape
    depth = norm_attn.shape[0]
    n_a = w_qkv_a.shape[0]
    d_ff = w_gate.shape[2]
    f_pad = -(-d_ff // FF_ALIGN) * FF_ALIGN
    gw = A_HG * HD_A
    nq = A_GROUPS * gw
    kw = KV_B * HD_B
    mp = nbp * seq
    ms = nbs * t_len
    tm_p = 1024
    tiles_per_seq = seq // tm_p
    a_caches = (cache_a1, cache_a2, cache_a3)
    shift_a = HD_A // ROT_FRAC // 2
    shift_b = HD_B // ROT_FRAC // 2

    pos_p = jnp.arange(seq, dtype=jnp.int32)
    pos_s = jnp.repeat(PAST_LEN + jnp.arange(t_len, dtype=jnp.int32), nbs)
    tabs_a_p, tabs_a_s = rope_tables(pos_p, HD_A), rope_tables(pos_s, HD_A)
    tabs_b_p, tabs_b_s = rope_tables(pos_p, HD_B), rope_tables(pos_s, HD_B)

    hp = x_prompt.reshape(mp, d)
    hs = x_sample.transpose(1, 0, 2).reshape(ms, d)

    dq = w_q_b.shape[2]
    wqkv_a = cast_pad(w_qkv_a, d, 3 * nq)
    wkv_b = cast_pad(w_kv_b[None], d, 2 * kw)
    cw = _pad_last(conv_w, f_pad)
    cb = _pad_last(conv_b[:, None], f_pad)
    ffn_src = {"gate": (w_gate, d, f_pad), "up": (w_up, d, f_pad), "down": (w_down, f_pad, d),
               "o_a": (w_o_a, gw, d), "q_b": (w_q_b, d, dq), "o_b": (w_o_b, dq, d)}
    ffn_w = {}

    def ride_arg(*candidates):
        for name, lyr in candidates:
            if lyr < ffn_src[name][0].shape[0] and (name, lyr) not in ffn_w:
                return (name, lyr), (ffn_src[name][0], lyr) + ffn_src[name][1:]
        return None, None

    def ffn_weight(name, lyr):
        if (name, lyr) not in ffn_w:
            ffn_w[name, lyr] = cast_pad(*ffn_src[name], layer=lyr)
        return ffn_w[name, lyr]

    new_a_p = [[] for _ in range(A_GROUPS)]
    new_a_s = [[] for _ in range(A_GROUPS)]
    conv_p, conv_s = [], []
    kv_p = kv_s = new_b_s = None
    pending_shifts = []

    for layer in range(depth):
        if layer < n_a:
            xp, = rmsnorm(hp, norm_attn[layer][None], BF16, 512)
            xs, = rmsnorm(hs, norm_attn[layer][None], BF16, ms)
            qkv_groups = []
            for gi, (win, dil) in enumerate(A_CONFIGS):
                ride_key, ride_w = ride_arg(("gate", layer), ("up", layer), ("down", layer))
                qkv_g, kvf_g, *cast = proj(xp, wqkv_a, layer, tabs_a_p, segs=[gi * gw, nq + gi * gw, 2 * nq + gi * gw],
                                           seg_cols=gw, tm=tm_p, tn=512, tiles_per_seq=tiles_per_seq,
                                           n_rope_segs=2, f32_from_seg=1, shift=shift_a, dil=dil, ride_w=ride_w)
                if cast:
                    ffn_w[ride_key] = cast[0][None]
                qkv_groups.append(qkv_g.reshape(nbp, dil, seq // dil, 3 * gw))
                new_a_p[gi].append(kvf_g.reshape(nbp, seq, 2, A_HG, HD_A)[:, -min(win, seq):])
            qkv_s = proj(xs, wqkv_a, layer, tabs_a_s, segs=[0, nq, 2 * nq], seg_cols=nq, tm=ms, tn=512, tiles_per_seq=1,
                         n_rope_segs=2, f32_from_seg=0, shift=shift_a, has_bf=False)
            qkv_s = qkv_s.reshape(t_len, nbs, 3, A_GROUPS, A_HG, HD_A).transpose(2, 1, 0, 3, 4, 5)
            as_ = attn_a_sample(qkv_s[0], qkv_s[1], qkv_s[2], [c[layer] for c in a_caches], t_len=t_len, nb=nbs)
            as_ = as_.transpose(1, 0, 2, 3).reshape(ms, gw).astype(BF16)
            views = {}
            for gi in range(A_GROUPS):
                cache = a_caches[gi][layer]
                kv_new = jnp.stack([qkv_s[1][:, :, gi], qkv_s[2][:, :, gi]], axis=2)
                rows = cache.shape[1]
                new_a_s[gi].append(None)
                if rows % t_len == 0:
                    views[gi] = (cache.reshape((nbs, rows // t_len, t_len) + cache.shape[2:]), kv_new[:, None])
                else:
                    new_a_s[gi][-1] = jnp.concatenate([cache, kv_new], axis=1)[:, t_len:]
            by_size = sorted(views, key=lambda g: -views[g][0].shape[1])
            b_steps = B_SHIFT_SPLIT * nbp * (seq // BLOCK)
            if by_size and layer == n_a - 1 and depth > n_a and _ShiftRide.fits(views[by_size[0]][0], SHIFT_ROWS_B, b_steps):
                big = by_size.pop(0)
                pending_shifts.append((big, layer) + views[big])
            shifts, owner = {}, {}
            for gi in by_size:
                for host in sorted(range(A_GROUPS), key=lambda g: A_CONFIGS[g][1]):
                    if host not in shifts and _ShiftRide.fits(views[gi][0], SHIFT_ROWS, _attn_a_steps(host, nbp, seq)):
                        shifts[host], owner[host] = views[gi] + (SHIFT_ROWS,), gi
                        break
                else:
                    new_a_s[gi][-1] = jnp.concatenate([a_caches[gi][layer], views[gi][1][:, 0]], axis=1)[:, t_len:]
            cast_keys, casts = {}, {}
            for host, cands in ((1, [("o_a", layer)]), (2, [("gate", layer + 1)])):
                key, arg = ride_arg(*cands)
                tiles = key and (arg[2] // CAST_TILE[0]) * (arg[3] // CAST_TILE[1])
                if key and tiles <= _attn_a_steps(host, nbp, seq):
                    cast_keys[host], casts[host] = key, arg
            ap, shifted, cast_out = attn_a_prompt(qkv_groups, nb=nbp, seq=seq, shifts=shifts, casts=casts)
            for host, arr in shifted.items():
                new_a_s[owner[host]][-1] = arr.reshape(a_caches[owner[host]][layer].shape)
            for host, arr in cast_out.items():
                ffn_w[cast_keys[host]] = arr[None]
            wo = ffn_weight("o_a", layer)
            o_ride_key, o_ride_w = ride_arg(("q_b", 0)) if layer == n_a - 1 else (None, None)
        else:
            lb = layer - n_a
            o_ride_key = o_ride_w = None
            if layer == n_a:
                gains = jnp.stack([norm_attn[layer], norm_kv])
                xp, xkv_p = rmsnorm(hp, gains, BF16, 512)
                xs, xkv_s = rmsnorm(hs, gains, BF16, ms)
                kv_p = proj(xkv_p, wkv_b, 0, tabs_b_p, segs=[0, kw], seg_cols=kw, tm=tm_p, tn=kw,
                            tiles_per_seq=tiles_per_seq, n_rope_segs=1, f32_from_seg=0, shift=shift_b, has_bf=False)
                kv_s = proj(xkv_s, wkv_b, 0, tabs_b_s, segs=[0, kw], seg_cols=kw, tm=ms, tn=kw,
                            tiles_per_seq=1, n_rope_segs=1, f32_from_seg=0, shift=shift_b, has_bf=False)
                kv_s5 = kv_s.reshape(t_len, nbs, 2, KV_B, HD_B).transpose(1, 0, 2, 3, 4)
                new_b_s = jnp.concatenate([cache_b, kv_s5], axis=1)[:, t_len:]
            else:
                xp, = rmsnorm(hp, norm_attn[layer][None], BF16, 512)
                xs, = rmsnorm(hs, norm_attn[layer][None], BF16, ms)
            wq = ffn_weight("q_b", lb)
            ride_key, ride_w = ride_arg(("o_b", lb))
            q_p = proj(xp, wq, 0, tabs_b_p, segs=[0], seg_cols=dq, tm=tm_p, tn=512, tiles_per_seq=tiles_per_seq,
                       n_rope_segs=1, f32_from_seg=None, shift=shift_b, ride_w=ride_w)
            if ride_w is not None:
                q_p, cast_w = q_p
                ffn_w[ride_key] = cast_w[None]
            q_s = proj(xs, wq, 0, tabs_b_s, segs=[0], seg_cols=dq, tm=ms, tn=512, tiles_per_seq=1,
                       n_rope_segs=1, f32_from_seg=None, shift=shift_b)
            host = pending_shifts.pop(0) if pending_shifts else None
            nsplit = B_SHIFT_SPLIT if host else 1
            ride_key, ride_w = ride_arg(("down", layer))
            if ride_w is not None and (f_pad // CAST_TILE[0]) * (d // CAST_TILE[1]) > nbp * (seq // BLOCK) * nsplit:
                ride_key = ride_w = None
            outs = attn_b(q_p.reshape(nbp, seq, dq), kv_p.reshape(nbp, seq, 2 * kw), None, sinks_b[lb],
                          nb=nbp, nblk=seq // BLOCK, tq=BLOCK, stack=1, first_block_has_prev=False,
                          nsplit=nsplit, shift=host[2:] + (SHIFT_ROWS_B,) if host else None, cast_w=ride_w)
            outs = list(outs) if isinstance(outs, (list, tuple)) else [outs]
            ap = outs.pop(0).reshape(mp, dq)
            if host:
                new_a_s[host[0]][host[1]] = outs.pop(0).reshape(a_caches[host[0]][host[1]].shape)
            if ride_w is not None:
                ffn_w[ride_key] = outs.pop(0)[None]
            wo = ffn_weight("o_b", lb)
            npair = dq // LANES // KV_B
            q_sb = q_s.reshape(t_len, nbs, KV_B, npair, LANES).transpose(1, 3, 0, 2, 4)
            q_sb = q_sb.reshape(nbs, npair * t_len, KV_B * LANES)
            kv_sb = jnp.pad(kv_s.reshape(t_len, nbs, 2 * kw).transpose(1, 0, 2), ((0, 0), (0, BLOCK - t_len), (0, 0)))
            o_sb = attn_b(q_sb, kv_sb, cache_b.reshape(nbs, WINDOW_B, 2 * kw), sinks_b[lb],
                          nb=nbs, nblk=1, tq=t_len, stack=npair, first_block_has_prev=True)
            as_ = o_sb.reshape(nbs, npair, t_len, KV_B, LANES).transpose(2, 0, 3, 1, 4).reshape(ms, dq)
        if o_ride_w is not None and (o_ride_w[2] // CAST_TILE[0]) * (o_ride_w[3] // CAST_TILE[1]) > (mp // 1024) * (d // 1024):
            o_ride_key = o_ride_w = None
        hp = mm_res(ap, wo, 0, hp, tm=1024, tn=1024, tk=wo.shape[1], ride_w=o_ride_w)
        if o_ride_w is not None:
            hp, cast_w = hp
            ffn_w[o_ride_key] = cast_w[None]
        hs = mm_res(as_, wo, 0, hs, tm=ms, tn=1024, tk=wo.shape[1])

        xp, = rmsnorm(hp, norm_ffn[layer][None], BF16, 512)
        xs, = rmsnorm(hs, norm_ffn[layer][None], BF16, ms)
        wg, wu = ffn_weight("gate", layer), ffn_weight("up", layer)
        ride_key, ride_w = ride_arg(("gate", layer + 1), ("down", layer))
        act_p, tails, *cast = ffn_up_prompt(xp, wg, wu, 0, cw, cb, layer, tm=tm_p, tf=512,
                                            tiles_per_seq=tiles_per_seq, ride_w=ride_w)
        if cast:
            ffn_w[ride_key] = cast[0][None]
        prev_s = _pad_last(state_conv[layer].transpose(1, 0, 2).reshape((CONV_W - 1) * nbs, d_ff), f_pad)
        act_s, tail_s = ffn_up_sample(xs, wg, wu, 0, cw, cb, layer, prev_s, tf=1024, nb=nbs)
        conv_p.append(tails.reshape(nbp, tiles_per_seq, 8, f_pad)[:, -1, 8 - (CONV_W - 1):, :d_ff])
        conv_s.append(tail_s.reshape(CONV_W - 1, nbs, f_pad).transpose(1, 0, 2)[:, :, :d_ff])
        tk_down = f_pad // 4
        wd = ffn_weight("down", layer)
        ride_key, ride_w = ride_arg(("up", layer + 1))
        hp = mm_res(act_p, wd, 0, hp, tm=1024, tn=1024, tk=tk_down, ride_w=ride_w)
        if ride_w is not None:
            hp, cast_w = hp
            ffn_w[ride_key] = cast_w[None]
        hs = mm_res(act_s, wd, 0, hs, tm=ms, tn=1024, tk=tk_down)

    for gi, lyr, _, new in pending_shifts:
        new_a_s[gi][lyr] = jnp.concatenate([a_caches[gi][lyr], new[:, 0]], axis=1)[:, t_len:]

    y_p, = rmsnorm(hp, norm_final[None], F32, 512)
    y_s, = rmsnorm(hs, norm_final[None], F32, ms)
    y_prompt = y_p.reshape(nbp, seq, d)
    y_sample = y_s.reshape(t_len, nbs, d).transpose(1, 0, 2)

    new_b_p = kv_p.reshape(nbp, seq, 2, KV_B, HD_B)[:, -min(WINDOW_B, seq):]
    outs = [y_prompt, y_sample]
    for gi in range(A_GROUPS):
        outs += [jnp.stack(new_a_p[gi], axis=0), jnp.stack(new_a_s[gi], axis=0)]
    outs += [new_b_p, new_b_s, jnp.stack(conv_p, axis=0), jnp.stack(conv_s, axis=0)]
    return tuple(outs)
```

```python
import functools

import jax
import jax.numpy as jnp
from jax import lax
from jax.experimental import pallas as pl
from jax.experimental.pallas import tpu as pltpu

F32 = jnp.float32
BF16 = jnp.bfloat16

HD_A = 128
A_HG = 16
A_CONFIGS = ((128, 1), (512, 4), (2048, 16))
A_GROUPS = len(A_CONFIGS)
A_LSE_CHUNK = 8
HD_B = 64
KV_B = 8
WINDOW_B = 128
PAST_LEN = 16384
CONV_W = 3
ROPE_THETA = 500000.0
ROT_FRAC = 4
NORM_EPS = 1e-5
BLOCK = 128
NEG_INF = -1e30
LANES = 128
FF_ALIGN = 1024
FFN_ROWS = 64
CAST_TILE = (512, 1024)
CAST_CHUNK = (128, 128)
SHIFT_ROWS = 256
SHIFT_ROWS_B = 512
B_SHIFT_SPLIT = 2
VMEM_LIMIT = 60 * 1024 * 1024


def _params(*sem):
    return pltpu.CompilerParams(dimension_semantics=sem, vmem_limit_bytes=VMEM_LIMIT)


def _div_pow2(x, n):
    assert n & (n - 1) == 0
    return x >> (n.bit_length() - 1)


def _mod_pow2(x, n):
    assert n & (n - 1) == 0
    return x & (n - 1)


def _cast_tile(x, row0, col0, rows, cols, padded):
    if padded:
        r = lax.broadcasted_iota(jnp.int32, x.shape, 0) + row0
        c = lax.broadcasted_iota(jnp.int32, x.shape, 1) + col0
        x = jnp.where((r < rows) & (c < cols), x, 0.0)
    return x.astype(BF16)


class _Ride:
    def __init__(self, w, layer, rows_pad, cols_pad, lin, nsteps):
        self.w, self.layer, self.lin = w, layer, lin
        _, self.rows, self.cols = w.shape
        self.rows_pad, self.cols_pad = rows_pad, cols_pad
        self.ncol = cols_pad // CAST_TILE[1]
        self.ntile = (rows_pad // CAST_TILE[0]) * self.ncol
        assert rows_pad % CAST_TILE[0] == 0 and cols_pad % CAST_TILE[1] == 0 and nsteps >= self.ntile

    def tile(self, step):
        t = jnp.minimum(step, self.ntile - 1)
        return t // self.ncol, t % self.ncol

    @property
    def in_spec(self):
        return pl.BlockSpec((None,) + CAST_TILE, lambda *g: (self.layer,) + self.tile(self.lin(*g)))

    @property
    def out_spec(self):
        return pl.BlockSpec(CAST_TILE, lambda *g: self.tile(self.lin(*g)))

    @property
    def out_shape(self):
        return jax.ShapeDtypeStruct((self.rows_pad, self.cols_pad), BF16)

    def cast(self, w_ref, o_ref, step):
        ti, tj = self.tile(step)
        for r0 in range(0, CAST_TILE[0], CAST_CHUNK[0]):
            for c0 in range(0, CAST_TILE[1], CAST_CHUNK[1]):
                sl = (slice(r0, r0 + CAST_CHUNK[0]), slice(c0, c0 + CAST_CHUNK[1]))
                o_ref[sl] = _cast_tile(w_ref[sl], ti * CAST_TILE[0] + r0, tj * CAST_TILE[1] + c0, self.rows,
                                       self.cols, (self.rows_pad, self.cols_pad) != (self.rows, self.cols))


class _ShiftRide:
    def __init__(self, cache, new, block_rows, lin, nsteps):
        self.cache, self.new, self.lin = cache, new, lin
        nb, self.ngroups, t = cache.shape[:3]
        self.bg = block_rows // t
        self.nblk = self.ngroups // self.bg
        self.count = nb * self.nblk
        self.tail = cache.shape[2:]
        assert block_rows % t == 0 and self.ngroups % self.bg == 0 and self.bg > 1 and nsteps >= self.count

    @staticmethod
    def fits(cache, block_rows, nsteps):
        nb, ngroups, t = cache.shape[:3]
        bg = block_rows // t
        return block_rows % t == 0 and bg > 1 and ngroups % bg == 0 and nb * (ngroups // bg) <= nsteps

    def block(self, step):
        gid = jnp.minimum(step, self.count - 1)
        return gid // self.nblk, gid % self.nblk

    def _zeros(self):
        return (0,) * len(self.tail)

    @property
    def in_specs(self):
        def src(*g):
            b, i = self.block(self.lin(*g))
            return (b, jnp.minimum(i * self.bg + 1, self.ngroups - self.bg)) + self._zeros()

        return [pl.BlockSpec((pl.Element(1), pl.Element(self.bg)) + tuple(pl.Element(d) for d in self.tail), src),
                pl.BlockSpec((None, 1) + self.tail, lambda *g: (self.block(self.lin(*g))[0], 0) + self._zeros())]

    @property
    def args(self):
        return [self.cache, self.new]

    @property
    def out_spec(self):
        return pl.BlockSpec((None, self.bg) + self.tail, lambda *g: self.block(self.lin(*g)) + self._zeros())

    @property
    def out_shape(self):
        return jax.ShapeDtypeStruct(self.cache.shape, self.cache.dtype)

    def copy(self, in_ref, out_ref):
        out_ref[...] = in_ref[0]

    def fix_last(self, in_ref, new_ref, out_ref, step):
        _, i = self.block(step)

        @pl.when(i == self.nblk - 1)
        def _():
            out_ref[0:self.bg - 1] = in_ref[0, 1:self.bg]
            out_ref[self.bg - 1:self.bg] = new_ref[...]


def _rmsnorm_kernel(h_ref, g_ref, *o_refs):
    x = h_ref[...]
    y = x * lax.rsqrt(jnp.mean(x * x, axis=-1, keepdims=True) + NORM_EPS)
    for n, o_ref in enumerate(o_refs):
        o_ref[...] = (y * g_ref[n:n + 1, :]).astype(o_ref.dtype)


def rmsnorm(h, gains, out_dtype, tm):
    m, d = h.shape
    n = gains.shape[0]
    return pl.pallas_call(
        _rmsnorm_kernel,
        out_shape=[jax.ShapeDtypeStruct((m, d), out_dtype)] * n,
        grid=(m // tm,),
        in_specs=[pl.BlockSpec((tm, d), lambda i: (i, 0)),
                  pl.BlockSpec((n, d), lambda i: (0, 0))],
        out_specs=[pl.BlockSpec((tm, d), lambda i: (i, 0))] * n,
        compiler_params=_params("parallel"),
        name="rmsnorm",
    )(h, gains)


def rope_tables(pos, hd):
    rot = hd // ROT_FRAC
    half = rot // 2
    inv = ROPE_THETA ** (-jnp.arange(half, dtype=F32) * 2.0 / rot)
    ang = pos.astype(F32)[:, None] * inv[None, :]
    cos, sin = jnp.cos(ang), jnp.sin(ang)
    s = pos.shape[0]
    ones = jnp.ones((s, hd - rot), F32)
    zrest = jnp.zeros((s, hd - rot), F32)
    zhalf = jnp.zeros((s, half), F32)
    reps = LANES // hd
    cos_t = jnp.tile(jnp.concatenate([cos, cos, ones], axis=1), (1, reps))
    sa_t = jnp.tile(jnp.concatenate([-sin, zhalf, zrest], axis=1), (1, reps))
    sb_t = jnp.tile(jnp.concatenate([zhalf, sin, zrest], axis=1), (1, reps))
    return (jnp.stack([cos_t, jnp.ones_like(cos_t)]), jnp.stack([sa_t, jnp.zeros_like(sa_t)]),
            jnp.stack([sb_t, jnp.zeros_like(sb_t)]))


def _proj_kernel(x_ref, w_ref, cos_ref, sa_ref, sb_ref, *refs, shift, dil, has_bf, has_f32, ride):
    refs = list(refs)
    ride_w_ref = refs.pop(0) if ride else None
    obf_ref = refs.pop(0) if has_bf else None
    of32_ref = refs.pop(0) if has_f32 else None
    ride_o_ref = refs.pop(0) if ride else None
    acc_refs = (refs.pop(0), refs.pop(0))
    slab_ref = refs.pop(0) if dil > 1 else None
    s = pl.program_id(0)
    tm, tn = acc_refs[0].shape

    @pl.when(s == 0)
    def _():
        acc_refs[1][...] = jnp.zeros((tm, tn), F32)

    def finish(acc_ref):
        for c in range(tn // LANES):
            sl = slice(c * LANES, (c + 1) * LANES)
            xh = acc_ref[:, sl]
            xh = (xh * cos_ref[...] + pltpu.roll(xh, LANES - shift, 1) * sa_ref[...]
                  + pltpu.roll(xh, shift, 1) * sb_ref[...])
            if has_f32:
                of32_ref[:, sl] = xh
            if not has_bf:
                continue
            if dil == 1:
                obf_ref[:, sl] = xh.astype(obf_ref.dtype)
            else:
                slab_ref[...] = xh
                for r in range(dil):
                    obf_ref[r, :, sl] = slab_ref[pl.ds(r, tm // dil, stride=dil), :].astype(obf_ref.dtype)

    for parity in range(2):
        @pl.when(s % 2 == parity)
        def _(parity=parity):
            if ride:
                ride.cast(ride_w_ref, ride_o_ref, s)
            acc_refs[parity][...] = jnp.dot(x_ref[...], w_ref[...], preferred_element_type=F32)
            finish(acc_refs[1 - parity])


def proj(xn, w, layer, tables, *, segs, seg_cols, tm, tn, tiles_per_seq, n_rope_segs, f32_from_seg, shift,
         dil=1, has_bf=True, ride_w=None):
    m, k = xn.shape
    per_seg = seg_cols // tn
    nj = len(segs) * per_seg
    ncols = len(segs) * seg_cols
    starts = [s // tn for s in segs]
    step = starts[1] - starts[0] if len(segs) > 1 else 0
    assert all(starts[a] == starts[0] + a * step for a in range(len(segs)))
    has_f32 = f32_from_seg is not None
    f32_from = f32_from_seg * per_seg if has_f32 else nj
    n_rope = n_rope_segs * per_seg
    ntiles = (m // tm) * nj

    def cur(s):
        t = jnp.minimum(s, ntiles - 1)
        return t // nj, t % nj

    def prev(s):
        t = jnp.maximum(s - 1, 0)
        return t // nj, t % nj

    def w_map(s):
        _, j = cur(s)
        return layer, 0, starts[0] + (j // per_seg) * step + j % per_seg

    def tab_map(s):
        i, j = prev(s)
        return jnp.where(j < n_rope, 0, 1), i % tiles_per_seq, 0

    def bf_map(s):
        i, j = prev(s)
        return (i, j) if dil == 1 else (i // tiles_per_seq, 0, i % tiles_per_seq, j)

    def f32_map(s):
        i, j = prev(s)
        return i, jnp.maximum(j - f32_from, 0)

    out_shape, out_specs = [], []
    if has_bf:
        if dil == 1:
            out_shape.append(jax.ShapeDtypeStruct((m, ncols), BF16))
            out_specs.append(pl.BlockSpec((tm, tn), bf_map))
        else:
            nseq = m // (tm * tiles_per_seq)
            out_shape.append(jax.ShapeDtypeStruct((nseq, dil, tm * tiles_per_seq // dil, ncols), BF16))
            out_specs.append(pl.BlockSpec((None, dil, tm // dil, tn), bf_map))
    if has_f32:
        out_shape.append(jax.ShapeDtypeStruct((m, ncols - f32_from * tn), F32))
        out_specs.append(pl.BlockSpec((tm, tn), f32_map))
    tab_spec = pl.BlockSpec((None, tm, LANES), tab_map)
    in_specs = [pl.BlockSpec((tm, k), lambda s: (cur(s)[0], 0)),
                pl.BlockSpec((None, k, tn), w_map),
                tab_spec, tab_spec, tab_spec]
    args = [xn, w, *tables]
    ride = None
    if ride_w is not None:
        ride = _Ride(*ride_w, lin=lambda s: s, nsteps=ntiles + 1)
        in_specs.append(ride.in_spec)
        args.append(ride.w)
        out_shape.append(ride.out_shape)
        out_specs.append(ride.out_spec)
    outs = pl.pallas_call(
        functools.partial(_proj_kernel, shift=shift, dil=dil, has_bf=has_bf, has_f32=has_f32, ride=ride),
        out_shape=out_shape,
        grid=(ntiles + 1,),
        in_specs=in_specs,
        out_specs=out_specs,
        scratch_shapes=[pltpu.VMEM((tm, tn), F32), pltpu.VMEM((tm, tn), F32)]
        + ([pltpu.VMEM((tm, LANES), F32)] if dil > 1 else []),
        compiler_params=_params("arbitrary"),
        name="proj",
    )(*args)
    return outs if len(outs) > 1 else outs[0]


def _mm_res_kernel(x_ref, w_ref, res_ref, *refs, ride):
    refs = list(refs)
    ride_w_ref = refs.pop(0) if ride else None
    o_ref = refs.pop(0)

    @pl.when(pl.program_id(2) == 0)
    def _():
        o_ref[...] = res_ref[...]

    o_ref[...] += jnp.dot(x_ref[...], w_ref[...], preferred_element_type=F32)
    if ride:
        ride.cast(ride_w_ref, refs.pop(0), ride.lin(pl.program_id(0), pl.program_id(1), pl.program_id(2)))


def mm_res(x, w, layer, res, *, tm, tn, tk, ride_w=None):
    m, kdim = x.shape
    n = w.shape[2]
    grid = (m // tm, n // tn, kdim // tk)
    in_specs = [pl.BlockSpec((tm, tk), lambda i, j, k: (i, k)),
                pl.BlockSpec((None, tk, tn), lambda i, j, k: (layer, k, j)),
                pl.BlockSpec((tm, tn), lambda i, j, k: (i, j))]
    args = [x, w, res]
    out_shape = [jax.ShapeDtypeStruct((m, n), F32)]
    out_specs = [pl.BlockSpec((tm, tn), lambda i, j, k: (i, j))]
    ride = None
    if ride_w is not None:
        ride = _Ride(*ride_w, lin=lambda i, j, k: (i * grid[1] + j) * grid[2] + k, nsteps=grid[0] * grid[1] * grid[2])
        in_specs.append(ride.in_spec)
        args.append(ride.w)
        out_shape.append(ride.out_shape)
        out_specs.append(ride.out_spec)
    outs = pl.pallas_call(
        functools.partial(_mm_res_kernel, ride=ride),
        out_shape=out_shape,
        grid=grid,
        in_specs=in_specs,
        out_specs=out_specs,
        compiler_params=_params("parallel" if ride is None else "arbitrary",
                                "parallel" if ride is None else "arbitrary", "arbitrary"),
        name="mm_res",
    )(*args)
    return outs if ride else outs[0]


def _silu(c):
    return c * jax.nn.sigmoid(c)


def _ffn_up_prompt_kernel(x_ref, wg_ref, wu_ref, cw_ref, cb_ref, *refs, nj, tiles_per_seq, ride):
    refs = list(refs)
    ride_w_ref = refs.pop(0) if ride else None
    act_ref, tail_ref = refs.pop(0), refs.pop(0)
    ride_o_ref = refs.pop(0) if ride else None
    g0_ref, u0_ref, g1_ref, u1_ref, carry_ref = refs
    s = pl.program_id(0)
    t_prev = jnp.maximum(s - 1, 0)
    ip = t_prev // nj
    jp = t_prev % nj
    tm, tf = g0_ref.shape

    @pl.when(s == 0)
    def _():
        g1_ref[...] = jnp.zeros((tm, tf), F32)
        u1_ref[...] = jnp.zeros((tm, tf), F32)

    @pl.when((ip % tiles_per_seq) == 0)
    def _():
        carry_ref[jp] = jnp.zeros(carry_ref.shape[1:], F32)

    def finish(g_ref, u_ref):
        for c0 in range(0, tf, LANES):
            sl = slice(c0, c0 + LANES)
            cw0, cw1, cw2, cbv = cw_ref[0:1, sl], cw_ref[1:2, sl], cw_ref[2:3, sl], cb_ref[:, sl]
            prev = carry_ref[jp, :, sl]
            for a in range(0, tm, FFN_ROWS):
                if a == 0:
                    ext = jnp.concatenate([prev, g_ref[0:FFN_ROWS, sl]], axis=0)
                else:
                    ext = g_ref[a - 8:a + FFN_ROWS, sl]
                c = cbv + cw0 * pltpu.roll(ext, 2, 0)[8:] + cw1 * pltpu.roll(ext, 1, 0)[8:] + cw2 * ext[8:]
                act_ref[a:a + FFN_ROWS, sl] = (_silu(c) * u_ref[a:a + FFN_ROWS, sl]).astype(act_ref.dtype)
            last = g_ref[tm - 8:tm, sl]
            carry_ref[jp, :, sl] = last
            tail_ref[:, sl] = last

    bufs = ((g0_ref, u0_ref), (g1_ref, u1_ref))
    for parity in range(2):
        @pl.when(s % 2 == parity)
        def _(parity=parity):
            x = x_ref[...]
            bufs[parity][0][...] = jnp.dot(x, wg_ref[...], preferred_element_type=F32)
            bufs[parity][1][...] = jnp.dot(x, wu_ref[...], preferred_element_type=F32)
            finish(*bufs[1 - parity])
            if ride:
                ride.cast(ride_w_ref, ride_o_ref, s)


def ffn_up_prompt(xn, wg, wu, wlayer, cw, cb, layer, *, tm, tf, tiles_per_seq, ride_w=None):
    m, d = xn.shape
    f = wg.shape[2]
    nj = f // tf
    ntiles = (m // tm) * nj

    def cur(s):
        t = jnp.minimum(s, ntiles - 1)
        return t // nj, t % nj

    def prev(s):
        t = jnp.maximum(s - 1, 0)
        return t // nj, t % nj

    buf = pltpu.VMEM((tm, tf), F32)
    in_specs = [pl.BlockSpec((tm, d), lambda s: (cur(s)[0], 0)),
                pl.BlockSpec((None, d, tf), lambda s: (wlayer, 0, cur(s)[1])),
                pl.BlockSpec((None, d, tf), lambda s: (wlayer, 0, cur(s)[1])),
                pl.BlockSpec((None, CONV_W, tf), lambda s: (layer, 0, prev(s)[1])),
                pl.BlockSpec((None, 1, tf), lambda s: (layer, 0, prev(s)[1]))]
    args = [xn, wg, wu, cw, cb]
    out_shape = [jax.ShapeDtypeStruct((m, f), BF16), jax.ShapeDtypeStruct((m // tm, 8, f), F32)]
    out_specs = [pl.BlockSpec((tm, tf), lambda s: prev(s)),
                 pl.BlockSpec((None, 8, tf), lambda s: (prev(s)[0], 0, prev(s)[1]))]
    ride = None
    if ride_w is not None:
        ride = _Ride(*ride_w, lin=lambda s: s, nsteps=ntiles + 1)
        in_specs.append(ride.in_spec)
        args.append(ride.w)
        out_shape.append(ride.out_shape)
        out_specs.append(ride.out_spec)
    return pl.pallas_call(
        functools.partial(_ffn_up_prompt_kernel, nj=nj, tiles_per_seq=tiles_per_seq, ride=ride),
        out_shape=out_shape,
        grid=(ntiles + 1,),
        in_specs=in_specs,
        out_specs=out_specs,
        scratch_shapes=[buf, buf, buf, buf, pltpu.VMEM((nj, 8, tf), F32)],
        compiler_params=_params("arbitrary"),
        name="ffn_up_prompt",
    )(*args)


def _ffn_up_sample_kernel(x_ref, wg_ref, wu_ref, cw_ref, cb_ref, prev_ref, act_ref, tail_ref, *, nb):
    x = x_ref[...]
    g = jnp.dot(x, wg_ref[...], preferred_element_type=F32)
    u = jnp.dot(x, wu_ref[...], preferred_element_type=F32)
    m = g.shape[0]
    gext = jnp.concatenate([prev_ref[...], g], axis=0)
    c = (cb_ref[...] + cw_ref[0:1, :] * gext[0:m] + cw_ref[1:2, :] * gext[nb:nb + m]
         + cw_ref[2:3, :] * gext[2 * nb:2 * nb + m])
    act_ref[...] = (_silu(c) * u).astype(act_ref.dtype)
    tail_ref[...] = gext[m:, :]


def ffn_up_sample(xn, wg, wu, wlayer, cw, cb, layer, prev, *, tf, nb):
    m, d = xn.shape
    f = wg.shape[2]
    return pl.pallas_call(
        functools.partial(_ffn_up_sample_kernel, nb=nb),
        out_shape=[jax.ShapeDtypeStruct((m, f), BF16),
                   jax.ShapeDtypeStruct((2 * nb, f), F32)],
        grid=(f // tf,),
        in_specs=[pl.BlockSpec((m, d), lambda j: (0, 0)),
                  pl.BlockSpec((None, d, tf), lambda j: (wlayer, 0, j)),
                  pl.BlockSpec((None, d, tf), lambda j: (wlayer, 0, j)),
                  pl.BlockSpec((None, CONV_W, tf), lambda j: (layer, 0, j)),
                  pl.BlockSpec((None, 1, tf), lambda j: (layer, 0, j)),
                  pl.BlockSpec((2 * nb, tf), lambda j: (0, j))],
        out_specs=[pl.BlockSpec((m, tf), lambda j: (0, j)),
                   pl.BlockSpec((2 * nb, tf), lambda j: (0, j))],
        compiler_params=_params("parallel"),
        name="ffn_up_sample",
    )(xn, wg, wu, cw, cb, prev)


def _attn_a_prompt_kernel(*refs, dil, use_prev, has_in, final, scale, shift, rides):
    it = iter(refs)
    q_ref = next(it)
    kc_ref = next(it)
    kp_ref = next(it) if use_prev else None
    vc_ref = next(it)
    vp_ref = next(it) if use_prev else None
    oin_ref = next(it) if has_in else None
    lin_ref = next(it) if has_in else None
    if shift:
        shift_in_ref, shift_new_ref = next(it), next(it)
    ride_w_refs = [next(it) for _ in rides]
    oout_ref = next(it)
    lout_ref = None if final else next(it)

    mb = pl.program_id(2)
    r = pl.program_id(3)
    if shift:
        shift_out_ref = next(it)
        shift.copy(shift_in_ref, shift_out_ref)
    for ride, ride_w_ref in zip(rides, ride_w_refs):
        ride.cast(ride_w_ref, next(it), ride.lin(pl.program_id(0), pl.program_id(1), mb, r))
    rows = pl.ds(r, BLOCK, stride=dil) if dil > 1 else slice(None)
    nkeys = 2 * BLOCK if use_prev else BLOCK
    qi = lax.broadcasted_iota(jnp.int32, (BLOCK, nkeys), 0)
    kj = lax.broadcasted_iota(jnp.int32, (BLOCK, nkeys), 1)
    if use_prev:
        dist = qi + BLOCK - kj
        valid = (dist >= 0) & (dist <= BLOCK) & ((mb > 0) | (kj >= BLOCK))
    else:
        valid = qi >= kj
    lane = lax.broadcasted_iota(jnp.int32, (BLOCK, LANES), 1)
    nh = q_ref.shape[1] // HD_A
    nchunk = nh // A_LSE_CHUNK
    lse_tiles = [jnp.zeros((BLOCK, LANES), F32) for _ in range(nchunk)]
    lse_in_tiles = [lin_ref[c, rows, :] for c in range(nchunk)] if has_in else None
    for h in range(nh):
        hc, hl = divmod(h, A_LSE_CHUNK)
        sl = slice(h * HD_A, (h + 1) * HD_A)
        q = q_ref[:, sl]
        if use_prev:
            k = jnp.concatenate([kp_ref[:, sl], kc_ref[:, sl]], axis=0)
            v = jnp.concatenate([vp_ref[:, sl], vc_ref[:, sl]], axis=0)
        else:
            k = kc_ref[:, sl]
            v = vc_ref[:, sl]
        s = lax.dot_general(q, k, (((1,), (1,)), ((), ())), preferred_element_type=F32) * scale
        s = jnp.where(valid, s, NEG_INF)
        m = jnp.max(s, axis=1, keepdims=True)
        p = jnp.exp(s - m)
        l = jnp.sum(p, axis=1, keepdims=True)
        o = jnp.dot(p.astype(BF16), v, preferred_element_type=F32) / l
        lse = m + jnp.log(l)
        if has_in:
            lse_in = lse_in_tiles[hc][:, hl:hl + 1]
            top = jnp.maximum(lse_in, lse)
            lse_new = top + jnp.log(jnp.exp(lse_in - top) + jnp.exp(lse - top))
            o = jnp.exp(lse_in - lse_new) * oin_ref[h, rows, :] + jnp.exp(lse - lse_new) * o
            lse = lse_new
        if final:
            oout_ref[:, sl] = o.astype(oout_ref.dtype)
        else:
            oout_ref[h, rows, :] = o
            lse_tiles[hc] = jnp.where(lane == hl, lse, lse_tiles[hc])
    if not final:
        for c in range(nchunk):
            lout_ref[c, rows, :] = lse_tiles[c]
    if shift:
        shift.fix_last(shift_in_ref, shift_new_ref, shift_out_ref,
                       shift.lin(pl.program_id(0), pl.program_id(1), mb, r))


def _attn_a_steps(gi, nb, seq):
    dil = A_CONFIGS[gi][1]
    nh = A_HG if BLOCK * dil <= 512 else A_LSE_CHUNK
    return nb * (A_HG // nh) * (seq // dil // BLOCK) * dil


def attn_a_prompt(qkv_groups, *, nb, seq, shifts=None, casts=None):
    m_rows = nb * seq
    gw = A_HG * HD_A
    o_state = lse_state = None
    order = sorted(range(A_GROUPS), key=lambda g: -A_CONFIGS[g][1])
    shifted, cast_out = {}, {}
    for step, gi in enumerate(order):
        win, dil = A_CONFIGS[gi]
        assert win // dil == BLOCK and seq % (dil * BLOCK) == 0
        nblk = seq // dil // BLOCK
        use_prev = nblk > 1
        has_in = step > 0
        final = step == A_GROUPS - 1
        assert not final or dil == 1
        qkv = qkv_groups[gi]
        nh = A_HG if BLOCK * dil <= 512 else A_LSE_CHUNK
        cw = nh * HD_A
        nhc = A_HG // nh
        nchunk = nh // A_LSE_CHUNK

        def seg_spec(seg, prev):
            if prev:
                return pl.BlockSpec((None, None, BLOCK, cw),
                                    lambda b, hc, mb, r, seg=seg: (b, r, jnp.maximum(mb - 1, 0), seg * nhc + hc))
            return pl.BlockSpec((None, None, BLOCK, cw), lambda b, hc, mb, r, seg=seg: (b, r, mb, seg * nhc + hc))

        o_spec = pl.BlockSpec((nh, BLOCK * dil, LANES), lambda b, hc, mb, r: (hc, b * nblk + mb, 0))
        l_spec = pl.BlockSpec((nchunk, BLOCK * dil, LANES), lambda b, hc, mb, r: (hc, b * nblk + mb, 0))
        args = [qkv, qkv]
        in_specs = [seg_spec(0, False), seg_spec(1, False)]
        if use_prev:
            args.append(qkv)
            in_specs.append(seg_spec(1, True))
        args.append(qkv)
        in_specs.append(seg_spec(2, False))
        if use_prev:
            args.append(qkv)
            in_specs.append(seg_spec(2, True))
        if has_in:
            args += [o_state, lse_state]
            in_specs += [o_spec, l_spec]
        if final:
            out_shape = [jax.ShapeDtypeStruct((m_rows, gw), BF16)]
            out_specs = [pl.BlockSpec((BLOCK, cw), lambda b, hc, mb, r: (b * nblk + mb, hc))]
        else:
            out_shape = [jax.ShapeDtypeStruct((A_HG, m_rows, LANES), F32),
                         jax.ShapeDtypeStruct((A_HG // A_LSE_CHUNK, m_rows, LANES), F32)]
            out_specs = [o_spec, l_spec]
        grid = (nb, nhc, nblk, dil)
        ride = None
        lin = lambda b, hc, mb, r, g=grid: ((b * g[1] + hc) * g[2] + mb) * g[3] + r
        nsteps = grid[0] * grid[1] * grid[2] * grid[3]
        if shifts and gi in shifts:
            ride = _ShiftRide(*shifts[gi], lin=lin, nsteps=nsteps)
            args += ride.args
            in_specs += ride.in_specs
        rides = [_Ride(*c, lin=lin, nsteps=nsteps) for c in (casts or {}).get(gi, [])]
        args += [c.w for c in rides]
        in_specs += [c.in_spec for c in rides]
        if ride is not None:
            out_shape.append(ride.out_shape)
            out_specs.append(ride.out_spec)
        out_shape += [c.out_shape for c in rides]
        out_specs += [c.out_spec for c in rides]
        seq_sem = "parallel" if ride is None and not rides else "arbitrary"
        outs = pl.pallas_call(
            functools.partial(_attn_a_prompt_kernel, dil=dil, use_prev=use_prev, has_in=has_in, final=final,
                              scale=HD_A ** -0.5, shift=ride, rides=rides),
            out_shape=out_shape,
            grid=grid,
            in_specs=in_specs,
            out_specs=out_specs,
            compiler_params=_params(seq_sem, seq_sem, "arbitrary", "arbitrary"),
            name=f"attn_a_prompt_g{gi}",
        )(*args)
        if rides:
            cast_out[gi] = outs[len(outs) - len(rides):]
            outs = outs[:len(outs) - len(rides)]
        if ride is not None:
            shifted[gi] = outs[-1]
            outs = outs[:-1]
        if final:
            return outs[0], shifted, cast_out
        o_state, lse_state = outs


def _attn_a_sample_kernel(q_ref, kn_ref, vn_ref, c1_ref, c2_ref, c3_ref, o_ref, *, t_len, scale):
    nh = A_HG
    cache_refs = (c1_ref, c2_ref, c3_ref)
    o_mix = [None] * t_len
    lse_mix = [None] * t_len
    for gi, (win, dil) in enumerate(A_CONFIGS):
        cref = cache_refs[gi]
        nkey = cref.shape[0]
        lb = nkey * dil
        t_sets = [list(range(t_len))] if dil == 1 else [[t] for t in range(t_len)]
        for ts in t_sets:
            nt = len(ts)
            res = (lb + ts[0]) % dil
            nrow = nt * nh
            qf = q_ref[ts[0]:ts[0] + nt, gi].reshape(nrow, HD_A).astype(BF16)
            kmat = cref[:, res, 0].reshape(nkey * nh, HD_A).astype(BF16)
            vmat = cref[:, res, 1].reshape(nkey * nh, HD_A).astype(BF16)
            s = lax.dot_general(qf, kmat, (((1,), (1,)), ((), ())), preferred_element_type=F32) * scale
            col = lax.broadcasted_iota(jnp.int32, s.shape, 1)
            rowi = lax.broadcasted_iota(jnp.int32, s.shape, 0)
            dist = lb + ts[0] + _div_pow2(rowi, nh) - (_div_pow2(col, nh) * dil + res)
            s = jnp.where((_mod_pow2(col, nh) == _mod_pow2(rowi, nh)) & (dist >= 0) & (dist <= win), s, NEG_INF)
            rowt = ts[0] + _div_pow2(lax.broadcasted_iota(jnp.int32, (nrow, 1), 0), nh)
            qf32 = qf.astype(F32)
            news = []
            for u in range(ts[-1] + 1):
                if not any((t - u) % dil == 0 and 0 <= t - u <= win for t in ts):
                    continue
                kn = jnp.tile(kn_ref[u, gi].astype(BF16).astype(F32), (nt, 1))
                ok = (rowt >= u) & (_mod_pow2(rowt - u, dil) == 0) & (rowt - u <= win)
                sn = jnp.where(ok, jnp.sum(qf32 * kn, axis=1, keepdims=True) * scale, NEG_INF)
                news.append((u, sn))
            m = jnp.max(s, axis=1, keepdims=True)
            for _, sn in news:
                m = jnp.maximum(m, sn)
            p = jnp.exp(s - m)
            l = jnp.sum(p, axis=1, keepdims=True)
            o = jnp.dot(p.astype(BF16), vmat, preferred_element_type=F32)
            for u, sn in news:
                pn = jnp.exp(sn - m)
                l = l + pn
                vn = jnp.tile(vn_ref[u, gi].astype(BF16).astype(F32), (nt, 1))
                o = o + pn.astype(BF16).astype(F32) * vn
            o = o / l
            lse = m + jnp.log(l)
            for a, t in enumerate(ts):
                o_t = o[a * nh:(a + 1) * nh]
                lse_t = lse[a * nh:(a + 1) * nh]
                if o_mix[t] is None:
                    o_mix[t], lse_mix[t] = o_t, lse_t
                else:
                    top = jnp.maximum(lse_mix[t], lse_t)
                    lse_new = top + jnp.log(jnp.exp(lse_mix[t] - top) + jnp.exp(lse_t - top))
                    o_mix[t] = jnp.exp(lse_mix[t] - lse_new) * o_mix[t] + jnp.exp(lse_t - lse_new) * o_t
                    lse_mix[t] = lse_new
    for t in range(t_len):
        o_ref[t] = o_mix[t]


def attn_a_sample(q, k_new, v_new, caches, *, t_len, nb):
    cache_args = []
    cache_specs = []
    for (win, dil), c in zip(A_CONFIGS, caches):
        lb = c.shape[1]
        assert lb % dil == 0 and (dil == 1 or t_len <= dil)
        nres = min(dil, t_len)
        cache_args.append(c.reshape(nb, lb // dil, dil, 2, A_HG, HD_A))
        cache_specs.append(pl.BlockSpec((None, lb // dil, nres, 2, A_HG, HD_A), lambda b: (b, 0, 0, 0, 0, 0)))
    row_spec = pl.BlockSpec((None, t_len, A_GROUPS, A_HG, HD_A), lambda b: (b, 0, 0, 0, 0))
    return pl.pallas_call(
        functools.partial(_attn_a_sample_kernel, t_len=t_len, scale=HD_A ** -0.5),
        out_shape=jax.ShapeDtypeStruct((nb, t_len, A_HG, HD_A), F32),
        grid=(nb,),
        in_specs=[row_spec, row_spec, row_spec] + cache_specs,
        out_specs=pl.BlockSpec((None, t_len, A_HG, HD_A), lambda b: (b, 0, 0, 0)),
        compiler_params=_params("parallel"),
        name="attn_a_sample",
    )(q, k_new, v_new, *cache_args)


def _attn_b_kernel(q_ref, kc_ref, vc_ref, kp_ref, vp_ref, sink_ref, *refs, scale, tq, stack,
                   first_block_has_prev, shift, ride):
    refs = list(refs)
    if shift:
        shift_in_ref, shift_new_ref = refs.pop(0), refs.pop(0)
    ride_w_ref = refs.pop(0) if ride else None
    o_ref = refs.pop(0)
    mb = pl.program_id(1)
    part = pl.program_id(2)
    if shift:
        shift_out_ref = refs.pop(0)
        shift.copy(shift_in_ref, shift_out_ref)
    if ride:
        ride.cast(ride_w_ref, refs.pop(0), ride.lin(pl.program_id(0), mb, part))
    nrow = stack * tq
    kvs = kc_ref.shape[1] // HD_B
    k_all = jnp.concatenate([kp_ref[...], kc_ref[...]], axis=0)
    v_all = jnp.concatenate([vp_ref[...], vc_ref[...]], axis=0)
    nk = k_all.shape[0]
    rowi = lax.broadcasted_iota(jnp.int32, (nrow, nk), 0)
    kj = lax.broadcasted_iota(jnp.int32, (nrow, nk), 1)
    dist = _mod_pow2(rowi, tq) + BLOCK - kj
    valid = (dist >= 0) & (dist <= WINDOW_B)
    if not first_block_has_prev:
        valid = valid & ((mb > 0) | (kj >= BLOCK))
    lane = lax.broadcasted_iota(jnp.int32, (nk, LANES), 1)
    lane_q = lax.broadcasted_iota(jnp.int32, (nrow, LANES), 1)
    pair_of_row = _div_pow2(lax.broadcasted_iota(jnp.int32, (nrow, 1), 0), tq)
    tiles_per_kvh = q_ref.shape[1] // LANES // kvs
    for kvh in range(kvs):
        pc = (kvh * HD_B // LANES) * LANES
        kpair = k_all[:, pc:pc + LANES]
        vpair = v_all[:, pc:pc + LANES]
        if (kvh * HD_B) % LANES == 0:
            k_lo = jnp.where(lane < HD_B, kpair, 0.0)
            v_lo = jnp.where(lane < HD_B, vpair, 0.0)
            k_hi = pltpu.roll(k_lo, HD_B, 1)
            v_hi = pltpu.roll(v_lo, HD_B, 1)
        else:
            k_hi = jnp.where(lane >= HD_B, kpair, 0.0)
            v_hi = jnp.where(lane >= HD_B, vpair, 0.0)
            k_lo = pltpu.roll(k_hi, HD_B, 1)
            v_lo = pltpu.roll(v_hi, HD_B, 1)
        k2 = jnp.concatenate([k_lo, k_hi], axis=0).astype(BF16)
        v2 = jnp.concatenate([v_lo, v_hi], axis=0).astype(BF16)
        for tile in range(tiles_per_kvh):
            col = (kvh * tiles_per_kvh + tile) * LANES
            q = q_ref[:, col:col + LANES]
            s = lax.dot_general(q, k2, (((1,), (1,)), ((), ())), preferred_element_type=F32) * scale
            ps = []
            inv = []
            for half in range(2):
                head0 = (((part * kvs + kvh) * tiles_per_kvh + tile) * stack) * 2 + half
                sink = jnp.full((nrow, 1), sink_ref[head0], F32)
                for pr in range(1, stack):
                    sink = jnp.where(pair_of_row == pr, sink_ref[head0 + 2 * pr], sink)
                sh = jnp.where(valid, s[:, half * nk:(half + 1) * nk], NEG_INF)
                m = jnp.maximum(jnp.max(sh, axis=1, keepdims=True), sink)
                p = jnp.exp(sh - m)
                l = jnp.sum(p, axis=1, keepdims=True) + jnp.exp(sink - m)
                ps.append(p)
                inv.append(1.0 / l)
            p2 = jnp.concatenate(ps, axis=1).astype(BF16)
            o = jnp.dot(p2, v2, preferred_element_type=F32)
            o_ref[:, col:col + LANES] = (o * jnp.where(lane_q < HD_B, inv[0], inv[1])).astype(o_ref.dtype)
    if shift:
        shift.fix_last(shift_in_ref, shift_new_ref, shift_out_ref, shift.lin(pl.program_id(0), mb, part))


def attn_b(q, kv_cur, kv_prev, sinks, *, nb, nblk, tq, stack, first_block_has_prev, nsplit=1, shift=None,
           cast_w=None):
    dq = q.shape[2] // nsplit
    wk = KV_B * HD_B // nsplit
    assert wk % LANES == 0 and dq % LANES == 0
    if kv_prev is None:
        prev_arg = kv_cur
        prev_row = lambda b, mb: (b, jnp.maximum(mb - 1, 0))
    else:
        assert nblk == 1
        prev_arg = kv_prev
        prev_row = lambda b, mb: (b, 0)
    in_specs = [pl.BlockSpec((None, stack * tq, dq), lambda b, mb, p: (b, mb, p)),
                pl.BlockSpec((None, BLOCK, wk), lambda b, mb, p: (b, mb, p)),
                pl.BlockSpec((None, BLOCK, wk), lambda b, mb, p: (b, mb, nsplit + p)),
                pl.BlockSpec((None, BLOCK, wk), lambda b, mb, p: prev_row(b, mb) + (p,)),
                pl.BlockSpec((None, BLOCK, wk), lambda b, mb, p: prev_row(b, mb) + (nsplit + p,)),
                pl.BlockSpec(memory_space=pltpu.SMEM)]
    args = [q, kv_cur, kv_cur, prev_arg, prev_arg, sinks]
    out_shape = [jax.ShapeDtypeStruct(q.shape, BF16)]
    out_specs = [pl.BlockSpec((None, stack * tq, dq), lambda b, mb, p: (b, mb, p))]
    lin = lambda b, mb, p: (b * nblk + mb) * nsplit + p
    ride = cast = None
    if shift is not None:
        ride = _ShiftRide(*shift, lin=lin, nsteps=nb * nblk * nsplit)
        args += ride.args
        in_specs += ride.in_specs
        out_shape.append(ride.out_shape)
        out_specs.append(ride.out_spec)
    if cast_w is not None:
        cast = _Ride(*cast_w, lin=lin, nsteps=nb * nblk * nsplit)
        args.append(cast.w)
        in_specs.append(cast.in_spec)
        out_shape.append(cast.out_shape)
        out_specs.append(cast.out_spec)
    outs = pl.pallas_call(
        functools.partial(_attn_b_kernel, scale=HD_B ** -0.5, tq=tq, stack=stack,
                          first_block_has_prev=first_block_has_prev, shift=ride, ride=cast),
        out_shape=out_shape,
        grid=(nb, nblk, nsplit),
        in_specs=in_specs,
        out_specs=out_specs,
        compiler_params=_params("parallel" if ride is None and cast is None else "arbitrary", "arbitrary",
                                "arbitrary"),
        name="attn_b",
    )(*args)
    return outs[0] if len(outs) == 1 else outs


def _cast_pad_kernel(w_ref, o_ref, *, rows, cols, padded):
    tr, tc = w_ref.shape
    o_ref[...] = _cast_tile(w_ref[...], pl.program_id(1) * tr, pl.program_id(2) * tc, rows, cols, padded)


def cast_pad(w, rows_pad, cols_pad, layer=None):
    nl, rows, cols = w.shape
    tr, tc = CAST_TILE
    if layer is not None:
        return pl.pallas_call(
            functools.partial(_cast_pad_kernel, rows=rows, cols=cols, padded=(rows_pad, cols_pad) != (rows, cols)),
            out_shape=jax.ShapeDtypeStruct((1, rows_pad, cols_pad), BF16),
            grid=(1, rows_pad // tr, cols_pad // tc),
            in_specs=[pl.BlockSpec((None, tr, tc), lambda l, i, j: (layer, i, j))],
            out_specs=pl.BlockSpec((None, tr, tc), lambda l, i, j: (0, i, j)),
            compiler_params=_params("parallel", "parallel", "parallel"),
            name="cast_pad",
        )(w)
    return pl.pallas_call(
        functools.partial(_cast_pad_kernel, rows=rows, cols=cols, padded=(rows_pad, cols_pad) != (rows, cols)),
        out_shape=jax.ShapeDtypeStruct((nl, rows_pad, cols_pad), BF16),
        grid=(nl, rows_pad // tr, cols_pad // tc),
        in_specs=[pl.BlockSpec((None, tr, tc), lambda l, i, j: (l, i, j))],
        out_specs=pl.BlockSpec((None, tr, tc), lambda l, i, j: (l, i, j)),
        compiler_params=_params("parallel", "parallel", "parallel"),
        name="cast_pad",
    )(w)


def _pad_last(w, n):
    return jnp.pad(w, [(0, 0)] * (w.ndim - 1) + [(0, n - w.shape[-1])])


def kernel(x_prompt, x_sample, cache_a1, cache_a2, cache_a3, cache_b, state_conv, norm_attn, norm_ffn,
           w_qkv_a, w_o_a, norm_kv, w_kv_b, w_q_b, sinks_b, w_o_b, w_gate, w_up, conv_w, conv_b, w_down,
           norm_final):
    nbp, seq, d = x_prompt.shape
    nbs, t_len, _ = x_sample.shape
    depth = norm_attn.shape[0]
    n_a = w_qkv_a.shape[0]
    d_ff = w_gate.shape[2]
    f_pad = -(-d_ff // FF_ALIGN) * FF_ALIGN
    gw = A_HG * HD_A
    nq = A_GROUPS * gw
    kw = KV_B * HD_B
    mp = nbp * seq
    ms = nbs * t_len
    tm_p = 1024
    tiles_per_seq = seq // tm_p
    a_caches = (cache_a1, cache_a2, cache_a3)
    shift_a = HD_A // ROT_FRAC // 2
    shift_b = HD_B // ROT_FRAC // 2

    pos_p = jnp.arange(seq, dtype=jnp.int32)
    pos_s = jnp.repeat(PAST_LEN + jnp.arange(t_len, dtype=jnp.int32), nbs)
    tabs_a_p, tabs_a_s = rope_tables(pos_p, HD_A), rope_tables(pos_s, HD_A)
    tabs_b_p, tabs_b_s = rope_tables(pos_p, HD_B), rope_tables(pos_s, HD_B)

    hp = x_prompt.reshape(mp, d)
    hs = x_sample.transpose(1, 0, 2).reshape(ms, d)

    dq = w_q_b.shape[2]
    wqkv_a = cast_pad(w_qkv_a, d, 3 * nq)
    wkv_b = cast_pad(w_kv_b[None], d, 2 * kw)
    cw = _pad_last(conv_w, f_pad)
    cb = _pad_last(conv_b[:, None], f_pad)
    ffn_src = {"gate": (w_gate, d, f_pad), "up": (w_up, d, f_pad), "down": (w_down, f_pad, d),
               "o_a": (w_o_a, gw, d), "q_b": (w_q_b, d, dq), "o_b": (w_o_b, dq, d)}
    ffn_w = {}

    def ride_arg(*candidates):
        for name, lyr in candidates:
            if lyr < ffn_src[name][0].shape[0] and (name, lyr) not in ffn_w:
                return (name, lyr), (ffn_src[name][0], lyr) + ffn_src[name][1:]
        return None, None

    def ffn_weight(name, lyr):
        if (name, lyr) not in ffn_w:
            ffn_w[name, lyr] = cast_pad(*ffn_src[name], layer=lyr)
        return ffn_w[name, lyr]

    new_a_p = [[] for _ in range(A_GROUPS)]
    new_a_s = [[] for _ in range(A_GROUPS)]
    conv_p, conv_s = [], []
    kv_p = kv_s = new_b_s = None
    pending_shifts = []

    for layer in range(depth):
        if layer < n_a:
            xp, = rmsnorm(hp, norm_attn[layer][None], BF16, 512)
            xs, = rmsnorm(hs, norm_attn[layer][None], BF16, ms)
            qkv_groups = []
            for gi, (win, dil) in enumerate(A_CONFIGS):
                ride_key, ride_w = ride_arg(("gate", layer), ("up", layer), ("down", layer))
                qkv_g, kvf_g, *cast = proj(xp, wqkv_a, layer, tabs_a_p, segs=[gi * gw, nq + gi * gw, 2 * nq + gi * gw],
                                           seg_cols=gw, tm=tm_p, tn=512, tiles_per_seq=tiles_per_seq,
                                           n_rope_segs=2, f32_from_seg=1, shift=shift_a, dil=dil, ride_w=ride_w)
                if cast:
                    ffn_w[ride_key] = cast[0][None]
                qkv_groups.append(qkv_g.reshape(nbp, dil, seq // dil, 3 * gw))
                new_a_p[gi].append(kvf_g.reshape(nbp, seq, 2, A_HG, HD_A)[:, -min(win, seq):])
            qkv_s = proj(xs, wqkv_a, layer, tabs_a_s, segs=[0, nq, 2 * nq], seg_cols=nq, tm=ms, tn=512, tiles_per_seq=1,
                         n_rope_segs=2, f32_from_seg=0, shift=shift_a, has_bf=False)
            qkv_s = qkv_s.reshape(t_len, nbs, 3, A_GROUPS, A_HG, HD_A).transpose(2, 1, 0, 3, 4, 5)
            as_ = attn_a_sample(qkv_s[0], qkv_s[1], qkv_s[2], [c[layer] for c in a_caches], t_len=t_len, nb=nbs)
            as_ = as_.transpose(1, 0, 2, 3).reshape(ms, gw).astype(BF16)
            views = {}
            for gi in range(A_GROUPS):
                cache = a_caches[gi][layer]
                kv_new = jnp.stack([qkv_s[1][:, :, gi], qkv_s[2][:, :, gi]], axis=2)
                rows = cache.shape[1]
                new_a_s[gi].append(None)
                if rows % t_len == 0:
                    views[gi] = (cache.reshape((nbs, rows // t_len, t_len) + cache.shape[2:]), kv_new[:, None])
                else:
                    new_a_s[gi][-1] = jnp.concatenate([cache, kv_new], axis=1)[:, t_len:]
            by_size = sorted(views, key=lambda g: -views[g][0].shape[1])
            b_steps = B_SHIFT_SPLIT * nbp * (seq // BLOCK)
            if by_size and layer == n_a - 1 and depth > n_a and _ShiftRide.fits(views[by_size[0]][0], SHIFT_ROWS_B, b_steps):
                big = by_size.pop(0)
                pending_shifts.append((big, layer) + views[big])
            shifts, owner = {}, {}
            for gi in by_size:
                for host in sorted(range(A_GROUPS), key=lambda g: A_CONFIGS[g][1]):
                    if host not in shifts and _ShiftRide.fits(views[gi][0], SHIFT_ROWS, _attn_a_steps(host, nbp, seq)):
                        shifts[host], owner[host] = views[gi] + (SHIFT_ROWS,), gi
                        break
                else:
                    new_a_s[gi][-1] = jnp.concatenate([a_caches[gi][layer], views[gi][1][:, 0]], axis=1)[:, t_len:]
            cast_keys, casts = {}, {}
            for host, cands in ((1, [("o_a", layer)]), (2, [("gate", layer + 1), ("down", layer + 1)])):
                for cand in cands:
                    key, arg = ride_arg(cand)
                    tiles = key and (arg[2] // CAST_TILE[0]) * (arg[3] // CAST_TILE[1])
                    if key and tiles <= _attn_a_steps(host, nbp, seq):
                        cast_keys.setdefault(host, []).append(key)
                        casts.setdefault(host, []).append(arg)
            ap, shifted, cast_out = attn_a_prompt(qkv_groups, nb=nbp, seq=seq, shifts=shifts, casts=casts)
            for host, arr in shifted.items():
                new_a_s[owner[host]][-1] = arr.reshape(a_caches[owner[host]][layer].shape)
            for host, arrs in cast_out.items():
                for key, arr in zip(cast_keys[host], arrs):
                    ffn_w[key] = arr[None]
            wo = ffn_weight("o_a", layer)
            o_ride_key, o_ride_w = ride_arg(("q_b", 0)) if layer == n_a - 1 else (None, None)
        else:
            lb = layer - n_a
            o_ride_key = o_ride_w = None
            if layer == n_a:
                gains = jnp.stack([norm_attn[layer], norm_kv])
                xp, xkv_p = rmsnorm(hp, gains, BF16, 512)
                xs, xkv_s = rmsnorm(hs, gains, BF16, ms)
                kv_p = proj(xkv_p, wkv_b, 0, tabs_b_p, segs=[0, kw], seg_cols=kw, tm=tm_p, tn=kw,
                            tiles_per_seq=tiles_per_seq, n_rope_segs=1, f32_from_seg=0, shift=shift_b, has_bf=False)
                kv_s = proj(xkv_s, wkv_b, 0, tabs_b_s, segs=[0, kw], seg_cols=kw, tm=ms, tn=kw,
                            tiles_per_seq=1, n_rope_segs=1, f32_from_seg=0, shift=shift_b, has_bf=False)
                kv_s5 = kv_s.reshape(t_len, nbs, 2, KV_B, HD_B).transpose(1, 0, 2, 3, 4)
                new_b_s = jnp.concatenate([cache_b, kv_s5], axis=1)[:, t_len:]
            else:
                xp, = rmsnorm(hp, norm_attn[layer][None], BF16, 512)
                xs, = rmsnorm(hs, norm_attn[layer][None], BF16, ms)
            wq = ffn_weight("q_b", lb)
            ride_key, ride_w = ride_arg(("o_b", lb))
            q_p = proj(xp, wq, 0, tabs_b_p, segs=[0], seg_cols=dq, tm=tm_p, tn=512, tiles_per_seq=tiles_per_seq,
                       n_rope_segs=1, f32_from_seg=None, shift=shift_b, ride_w=ride_w)
            if ride_w is not None:
                q_p, cast_w = q_p
                ffn_w[ride_key] = cast_w[None]
            q_s = proj(xs, wq, 0, tabs_b_s, segs=[0], seg_cols=dq, tm=ms, tn=512, tiles_per_seq=1,
                       n_rope_segs=1, f32_from_seg=None, shift=shift_b)
            host = pending_shifts.pop(0) if pending_shifts else None
            nsplit = B_SHIFT_SPLIT if host else 1
            ride_key, ride_w = ride_arg(("down", layer))
            if ride_w is not None and (f_pad // CAST_TILE[0]) * (d // CAST_TILE[1]) > nbp * (seq // BLOCK) * nsplit:
                ride_key = ride_w = None
            outs = attn_b(q_p.reshape(nbp, seq, dq), kv_p.reshape(nbp, seq, 2 * kw), None, sinks_b[lb],
                          nb=nbp, nblk=seq // BLOCK, tq=BLOCK, stack=1, first_block_has_prev=False,
                          nsplit=nsplit, shift=host[2:] + (SHIFT_ROWS_B,) if host else None, cast_w=ride_w)
            outs = list(outs) if isinstance(outs, (list, tuple)) else [outs]
            ap = outs.pop(0).reshape(mp, dq)
            if host:
                new_a_s[host[0]][host[1]] = outs.pop(0).reshape(a_caches[host[0]][host[1]].shape)
            if ride_w is not None:
                ffn_w[ride_key] = outs.pop(0)[None]
            wo = ffn_weight("o_b", lb)
            npair = dq // LANES // KV_B
            q_sb = q_s.reshape(t_len, nbs, KV_B, npair, LANES).transpose(1, 3, 0, 2, 4)
            q_sb = q_sb.reshape(nbs, npair * t_len, KV_B * LANES)
            kv_sb = jnp.pad(kv_s.reshape(t_len, nbs, 2 * kw).transpose(1, 0, 2), ((0, 0), (0, BLOCK - t_len), (0, 0)))
            o_sb = attn_b(q_sb, kv_sb, cache_b.reshape(nbs, WINDOW_B, 2 * kw), sinks_b[lb],
                          nb=nbs, nblk=1, tq=t_len, stack=npair, first_block_has_prev=True)
            as_ = o_sb.reshape(nbs, npair, t_len, KV_B, LANES).transpose(2, 0, 3, 1, 4).reshape(ms, dq)
        if o_ride_w is not None and (o_ride_w[2] // CAST_TILE[0]) * (o_ride_w[3] // CAST_TILE[1]) > (mp // 1024) * (d // 1024):
            o_ride_key = o_ride_w = None
        hp = mm_res(ap, wo, 0, hp, tm=1024, tn=1024, tk=wo.shape[1], ride_w=o_ride_w)
        if o_ride_w is not None:
            hp, cast_w = hp
            ffn_w[o_ride_key] = cast_w[None]
        hs = mm_res(as_, wo, 0, hs, tm=ms, tn=1024, tk=wo.shape[1])

        xp, = rmsnorm(hp, norm_ffn[layer][None], BF16, 512)
        xs, = rmsnorm(hs, norm_ffn[layer][None], BF16, ms)
        wg, wu = ffn_weight("gate", layer), ffn_weight("up", layer)
        ride_key, ride_w = ride_arg(("gate", layer + 1), ("down", layer))
        act_p, tails, *cast = ffn_up_prompt(xp, wg, wu, 0, cw, cb, layer, tm=tm_p, tf=512,
                                            tiles_per_seq=tiles_per_seq, ride_w=ride_w)
        if cast:
            ffn_w[ride_key] = cast[0][None]
        prev_s = _pad_last(state_conv[layer].transpose(1, 0, 2).reshape((CONV_W - 1) * nbs, d_ff), f_pad)
        act_s, tail_s = ffn_up_sample(xs, wg, wu, 0, cw, cb, layer, prev_s, tf=1024, nb=nbs)
        conv_p.append(tails.reshape(nbp, tiles_per_seq, 8, f_pad)[:, -1, 8 - (CONV_W - 1):, :d_ff])
        conv_s.append(tail_s.reshape(CONV_W - 1, nbs, f_pad).transpose(1, 0, 2)[:, :, :d_ff])
        tk_down = f_pad // 4
        wd = ffn_weight("down", layer)
        ride_key, ride_w = ride_arg(("up", layer + 1))
        hp = mm_res(act_p, wd, 0, hp, tm=1024, tn=1024, tk=tk_down, ride_w=ride_w)
        if ride_w is not None:
            hp, cast_w = hp
            ffn_w[ride_key] = cast_w[None]
        hs = mm_res(act_s, wd, 0, hs, tm=ms, tn=1024, tk=tk_down)

    for gi, lyr, _, new in pending_shifts:
        new_a_s[gi][lyr] = jnp.concatenate([a_caches[gi][lyr], new[:, 0]], axis=1)[:, t_len:]

    y_p, = rmsnorm(hp, norm_final[None], F32, 512)
    y_s, = rmsnorm(hs, norm_final[None], F32, ms)
    y_prompt = y_p.reshape(nbp, seq, d)
    y_sample = y_s.reshape(t_len, nbs, d).transpose(1, 0, 2)

    new_b_p = kv_p.reshape(nbp, seq, 2, KV_B, HD_B)[:, -min(WINDOW_B, seq):]
    outs = [y_prompt, y_sample]
    for gi in range(A_GROUPS):
        outs += [jnp.stack(new_a_p[gi], axis=0), jnp.stack(new_a_s[gi], axis=0)]
    outs += [new_b_p, new_b_s, jnp.stack(conv_p, axis=0), jnp.stack(conv_s, axis=0)]
    return tuple(outs)
```
